```python
import jax
import jax.numpy as jnp
from jax import lax
import numpy as np

D_MODEL = 2048
BATCH = 2
SEQ = 8192
DEPTH = 2

GRID_W = 64
CTX_LEN = 256
MIX_W = D_MODEL
HEAD_DIM = 128
ATTN_W = MIX_W // 2
N_HEADS = ATTN_W // HEAD_DIM
N_KV_HEADS = 2
GQA_GROUP = N_HEADS // N_KV_HEADS
KV_W = N_KV_HEADS * HEAD_DIM
LRU_W = MIX_W - ATTN_W
LRU_BLOCKS = 8
LRU_BW = LRU_W // LRU_BLOCKS
CONV_W = 4
RG_C = 8.0
ROPE_THETA = 10000.0
ROPE_AXIS_DIM = HEAD_DIM // 2
Q_BLOCK = 128
IN_COLS = ATTN_W + 2 * KV_W + 2 * LRU_W
D_FF = 5632
N_EXPERTS = 8
TOP_K = 2
D_EXPERT = 7168
MOE_BLOCK = 512
N_DENSE_LAYERS = (DEPTH + 1) // 2
N_MOE_LAYERS = DEPTH // 2
DEEPNORM_ALPHA = (2 * DEPTH) ** 0.25
DEEPNORM_BETA = (8 * DEPTH) ** -0.25
LN_EPS = 1e-6

kernel_name = 'hybrid_rglru_gqa_moe_diffusion_trunk'


def _layer_norm(x, g=None, b=None):
    xf = x.astype(jnp.float32)
    mu = jnp.mean(xf, axis=-1, keepdims=True)
    var = jnp.mean(jnp.square(xf - mu), axis=-1, keepdims=True)
    y = (xf - mu) * lax.rsqrt(var + LN_EPS)
    if g is not None:
        y = y * g + b
    return y.astype(x.dtype)


def _rms_norm(x, g):
    xf = x.astype(jnp.float32)
    y = xf * lax.rsqrt(jnp.mean(xf * xf, axis=-1, keepdims=True) + LN_EPS) * g
    return y.astype(x.dtype)


def _modulate(x, shift, scale):
    return _layer_norm(x) * (1 + scale) + shift


def _axial_rope(rows):
    row = jnp.repeat(jnp.arange(rows, dtype=jnp.float32), GRID_W)
    col = jnp.tile(jnp.arange(GRID_W, dtype=jnp.float32), rows)
    half = ROPE_AXIS_DIM // 2
    inv_freq = ROPE_THETA ** (-jnp.arange(half, dtype=jnp.float32) / half)
    ang = jnp.stack([row[:, None] * inv_freq, col[:, None] * inv_freq], axis=1)
    return jnp.cos(ang), jnp.sin(ang)


def _apply_rope(x, cos, sin):
    b, s, h, _ = x.shape
    xr = x.astype(jnp.float32).reshape(b, s, h, 2, 2, ROPE_AXIS_DIM // 2)
    x1, x2 = xr[..., 0, :], xr[..., 1, :]
    cs, sn = cos[None, :, None], sin[None, :, None]
    out = jnp.stack([x1 * cs - x2 * sn, x2 * cs + x1 * sn], axis=-2)
    return out.reshape(b, s, h, HEAD_DIM).astype(x.dtype)


def _attend(q, k, v):
    s = jnp.einsum('bqkgd,bskd->bkgqs', q, k).astype(jnp.float32)
    p = jax.nn.softmax(s, axis=-1).astype(v.dtype)
    return jnp.einsum('bkgqs,bskd->bqkgd', p, v)


def _latent_attention(q, k_all, v_all):
    b, s = q.shape[:2]
    nq = s // Q_BLOCK
    qb = jnp.moveaxis(q.reshape(b, nq, Q_BLOCK, N_KV_HEADS, GQA_GROUP, HEAD_DIM), 1, 0)
    ob = lax.map(lambda blk: _attend(blk, k_all, v_all), qb)
    return jnp.moveaxis(ob, 0, 1).reshape(b, s, ATTN_W)


def _centred_dwconv(u, w, bias):
    n = u.shape[1]
    pad_l = (CONV_W - 1) // 2
    up = jnp.pad(u, ((0, 0), (pad_l, CONV_W - 1 - pad_l), (0, 0)))
    out = bias
    for k in range(CONV_W):
        out = out + up[:, k:k + n] * w[k]
    return out


def _linear_scan(log_a, inp, h0, reverse):
    a = jnp.exp(log_a)
    if reverse:
        a, inp = a[:, ::-1], inp[:, ::-1]
    inp = inp.at[:, 0].add(a[:, 0] * h0)

    def combine(left, right):
        return left[0] * right[0], right[0] * left[1] + right[1]

    _, h = lax.associative_scan(combine, (a, inp), axis=1)
    return h[:, ::-1] if reverse else h


def _rglru(u, wa, ba, wx, bx, lam, h0, reverse):
    b, n, _ = u.shape
    ub = u.reshape(b, n, LRU_BLOCKS, LRU_BW)
    r = jax.nn.sigmoid(jnp.einsum('bsnc,ncd->bsnd', ub, wa).reshape(b, n, LRU_W).astype(jnp.float32) + ba)
    gi = jax.nn.sigmoid(jnp.einsum('bsnc,ncd->bsnd', ub, wx).reshape(b, n, LRU_W).astype(jnp.float32) + bx)
    log_a = -RG_C * r * jax.nn.softplus(-lam.astype(jnp.float32))
    inp = jnp.sqrt(-jnp.expm1(2.0 * log_a)) * gi * u.astype(jnp.float32)
    return _linear_scan(log_a, inp, h0, reverse)


def _split_proj(p):
    return jnp.split(p, [ATTN_W, ATTN_W + KV_W, ATTN_W + 2 * KV_W, ATTN_W + 2 * KV_W + LRU_W], axis=-1)


def _swiglu(h, wg, wu, wd):
    return (jax.nn.silu(h @ wg) * (h @ wu)) @ wd


def _moe(h, w_router, wg, wu, wd):
    lead = h.shape[:-1]
    tok = h.reshape(-1, D_MODEL)
    n = tok.shape[0]
    logits = (tok @ w_router).astype(jnp.float32)
    top_val, top_idx = lax.top_k(logits, TOP_K)
    gates = jax.nn.softmax(top_val, axis=-1)
    n_assign = n * TOP_K
    exp_a = top_idx.reshape(-1)
    tok_a = jnp.repeat(jnp.arange(n, dtype=jnp.int32), TOP_K)
    gate_a = gates.reshape(-1)
    order = jnp.argsort(exp_a)
    exp_s, tok_s, gate_s = exp_a[order], tok_a[order], gate_a[order]
    counts = jnp.zeros((N_EXPERTS,), jnp.int32).at[exp_a].add(1)
    starts = jnp.cumsum(counts) - counts
    padded = (counts + MOE_BLOCK - 1) // MOE_BLOCK * MOE_BLOCK
    pad_ends = jnp.cumsum(padded)
    pad_starts = pad_ends - padded
    dest = pad_starts[exp_s] + jnp.arange(n_assign, dtype=jnp.int32) - starts[exp_s]
    n_blocks = -(-n_assign // MOE_BLOCK) + N_EXPERTS
    buf = jnp.zeros((n_blocks * MOE_BLOCK, D_MODEL), tok.dtype).at[dest].set(tok[tok_s])
    block_start = jnp.arange(n_blocks, dtype=jnp.int32) * MOE_BLOCK
    block_expert = jnp.minimum(jnp.searchsorted(pad_ends, block_start, side='right'), N_EXPERTS - 1)

    def expert_block(args):
        xb, e = args
        return _swiglu(xb, wg[e], wu[e], wd[e])

    out_buf = lax.map(expert_block, (buf.reshape(n_blocks, MOE_BLOCK, D_MODEL), block_expert))
    out_buf = out_buf.reshape(-1, D_MODEL)
    y = jnp.zeros_like(tok).at[tok_s].add(gate_s[:, None].astype(tok.dtype) * out_buf[dest])
    return y.reshape(*lead, D_MODEL)


def _channel_mixer(i, h, ffn_w_gate, ffn_w_up, ffn_w_down, router_w, exp_w_gate, exp_w_up, exp_w_down):
    j = i // 2
    if i % 2 == 0:
        return _swiglu(h, ffn_w_gate[j], ffn_w_up[j], ffn_w_down[j])
    return _moe(h, router_w[j], exp_w_gate[j], exp_w_up[j], exp_w_down[j])


def setup_inputs(seed: int = 0) -> dict:
    key = jax.random.key(seed)
    ks = jax.random.split(key, 28)
    f32 = jnp.float32

    def nrm(k, shape, scale):
        return scale * jax.random.normal(k, shape, f32)

    u = jax.random.uniform(ks[15], (DEPTH, 2, LRU_W), f32, 0.9, 0.999)
    s = u ** (1.0 / RG_C)
    return {
        'x': nrm(ks[0], (BATCH, SEQ, D_MODEL), 1.0),
        'c': nrm(ks[1], (BATCH, D_MODEL), 1.0),
        'ctx': nrm(ks[2], (BATCH, CTX_LEN, D_MODEL), 1.0),
        'c_ctx': nrm(ks[3], (D_MODEL,), 1.0),
        'w_mod': nrm(ks[4], (DEPTH, D_MODEL, 6 * D_MODEL), D_MODEL ** -0.5),
        'b_mod': nrm(ks[5], (DEPTH, 6 * D_MODEL), 0.02),
        'w_in': nrm(ks[6], (DEPTH, D_MODEL, IN_COLS), D_MODEL ** -0.5),
        'q_gain': 1.0 + nrm(ks[7], (DEPTH, HEAD_DIM), 0.02),
        'k_gain': 1.0 + nrm(ks[8], (DEPTH, HEAD_DIM), 0.02),
        'conv_w': nrm(ks[9], (DEPTH, CONV_W, LRU_W), CONV_W ** -0.5),
        'conv_b': nrm(ks[10], (DEPTH, LRU_W), 0.02),
        'lru_wa': nrm(ks[11], (DEPTH, 2, LRU_BLOCKS, LRU_BW, LRU_BW), LRU_BW ** -0.5),
        'lru_ba': nrm(ks[12], (DEPTH, 2, LRU_W), 0.02),
        'lru_wx': nrm(ks[13], (DEPTH, 2, LRU_BLOCKS, LRU_BW, LRU_BW), LRU_BW ** -0.5),
        'lru_bx': nrm(ks[14], (DEPTH, 2, LRU_W), 0.02),
        'lru_lambda': jnp.log(s) - jnp.log1p(-s),
        'out_gain_attn': 1.0 + nrm(ks[16], (DEPTH, ATTN_W), 0.02),
        'out_gain_lru': 1.0 + nrm(ks[17], (DEPTH, LRU_W), 0.02),
        'w_out': nrm(ks[18], (DEPTH, MIX_W, D_MODEL), MIX_W ** -0.5 * DEEPNORM_BETA),
        'ln_gain': 1.0 + nrm(ks[19], (DEPTH, 2, D_MODEL), 0.02),
        'ln_bias': nrm(ks[20], (DEPTH, 2, D_MODEL), 0.02),
        'ffn_w_gate': nrm(ks[21], (N_DENSE_LAYERS, D_MODEL, D_FF), D_MODEL ** -0.5),
        'ffn_w_up': nrm(ks[22], (N_DENSE_LAYERS, D_MODEL, D_FF), D_MODEL ** -0.5),
        'ffn_w_down': nrm(ks[23], (N_DENSE_LAYERS, D_FF, D_MODEL), D_FF ** -0.5 * DEEPNORM_BETA),
        'router_w': nrm(ks[24], (N_MOE_LAYERS, D_MODEL, N_EXPERTS), D_MODEL ** -0.5),
        'exp_w_gate': nrm(ks[25], (N_MOE_LAYERS, N_EXPERTS, D_MODEL, D_EXPERT), D_MODEL ** -0.5),
        'exp_w_up': nrm(ks[26], (N_MOE_LAYERS, N_EXPERTS, D_MODEL, D_EXPERT), D_MODEL ** -0.5),
        'exp_w_down': nrm(ks[27], (N_MOE_LAYERS, N_EXPERTS, D_EXPERT, D_MODEL), D_EXPERT ** -0.5 * DEEPNORM_BETA),
    }


def reference(x, c, ctx, c_ctx, w_mod, b_mod, w_in, q_gain, k_gain, conv_w, conv_b,
              lru_wa, lru_ba, lru_wx, lru_bx, lru_lambda, out_gain_attn, out_gain_lru, w_out,
              ln_gain, ln_bias, ffn_w_gate, ffn_w_up, ffn_w_down, router_w, exp_w_gate, exp_w_up, exp_w_down):
    b, s, _ = x.shape
    n_ctx = ctx.shape[1]
    rows = s // GRID_W
    cos, sin = _axial_rope(rows)
    q_scale = HEAD_DIM ** -0.5
    zeros_state = jnp.zeros((b, LRU_W), jnp.float32)

    for i in range(DEPTH):
        last = i == DEPTH - 1
        mod_lat = (jax.nn.silu(c) @ w_mod[i] + b_mod[i])[:, None, :]
        mod_ctx = jax.nn.silu(c_ctx) @ w_mod[i] + b_mod[i]
        sh1, sc1, g1, sh2, sc2, g2 = jnp.split(mod_lat, 6, axis=-1)
        csh1, csc1, cg1, csh2, csc2, cg2 = jnp.split(mod_ctx, 6, axis=-1)

        q_l, k_l, v_l, u_l, gb_l = _split_proj(_modulate(x, sh1, sc1) @ w_in[i])
        q_c, k_c, v_c, u_c, gb_c = _split_proj(_modulate(ctx, csh1, csc1) @ w_in[i])

        q_l = _apply_rope(_rms_norm(q_l.reshape(b, s, N_HEADS, HEAD_DIM), q_gain[i]), cos, sin)
        k_l = _apply_rope(_rms_norm(k_l.reshape(b, s, N_KV_HEADS, HEAD_DIM), k_gain[i]), cos, sin)
        k_c = _rms_norm(k_c.reshape(b, n_ctx, N_KV_HEADS, HEAD_DIM), k_gain[i])
        v_l = v_l.reshape(b, s, N_KV_HEADS, HEAD_DIM)
        v_c = v_c.reshape(b, n_ctx, N_KV_HEADS, HEAD_DIM)
        k_all = jnp.concatenate([k_c, k_l], axis=1)
        v_all = jnp.concatenate([v_c, v_l], axis=1)
        att_l = _latent_attention(q_l.reshape(b, s, N_KV_HEADS, GQA_GROUP, HEAD_DIM) * q_scale, k_all, v_all)

        u_c = _centred_dwconv(u_c, conv_w[i], conv_b[i])
        u_l = _centred_dwconv(u_l, conv_w[i], conv_b[i])
        hc_f = _rglru(u_c, lru_wa[i, 0], lru_ba[i, 0], lru_wx[i, 0], lru_bx[i, 0], lru_lambda[i, 0], zeros_state, False)
        hc_b = _rglru(u_c, lru_wa[i, 1], lru_ba[i, 1], lru_wx[i, 1], lru_bx[i, 1], lru_lambda[i, 1], zeros_state, True)
        hl_f = _rglru(u_l, lru_wa[i, 0], lru_ba[i, 0], lru_wx[i, 0], lru_bx[i, 0], lru_lambda[i, 0], hc_f[:, -1], False)
        hl_b = _rglru(u_l, lru_wa[i, 1], lru_ba[i, 1], lru_wx[i, 1], lru_bx[i, 1], lru_lambda[i, 1], hc_b[:, 0], True)
        rec_l = ((hl_f + hl_b) * jax.nn.gelu(gb_l.astype(jnp.float32))).astype(x.dtype)

        mix_l = jnp.concatenate([_rms_norm(att_l, out_gain_attn[i]), _rms_norm(rec_l, out_gain_lru[i])], axis=-1) @ w_out[i]
        x_mid = _layer_norm(DEEPNORM_ALPHA * x + g1 * mix_l, ln_gain[i, 0], ln_bias[i, 0])

        if not last:
            q_cc = _rms_norm(q_c.reshape(b, n_ctx, N_HEADS, HEAD_DIM), q_gain[i])
            att_c = _attend(q_cc.reshape(b, n_ctx, N_KV_HEADS, GQA_GROUP, HEAD_DIM) * q_scale, k_c, v_c)
            att_c = att_c.reshape(b, n_ctx, ATTN_W)
            rec_c = ((hc_f + hc_b) * jax.nn.gelu(gb_c.astype(jnp.float32))).astype(ctx.dtype)
            mix_c = jnp.concatenate([_rms_norm(att_c, out_gain_attn[i]), _rms_norm(rec_c, out_gain_lru[i])], axis=-1) @ w_out[i]
            ctx_mid = _layer_norm(DEEPNORM_ALPHA * ctx + cg1 * mix_c, ln_gain[i, 0], ln_bias[i, 0])

        f_l = _channel_mixer(i, _modulate(x_mid, sh2, sc2), ffn_w_gate, ffn_w_up, ffn_w_down,
                             router_w, exp_w_gate, exp_w_up, exp_w_down)
        x = _layer_norm(DEEPNORM_ALPHA * x_mid + g2 * f_l, ln_gain[i, 1], ln_bias[i, 1])
        if not last:
            f_c = _channel_mixer(i, _modulate(ctx_mid, csh2, csc2), ffn_w_gate, ffn_w_up, ffn_w_down,
                                 router_w, exp_w_gate, exp_w_up, exp_w_down)
            ctx = _layer_norm(DEEPNORM_ALPHA * ctx_mid + cg2 * f_c, ln_gain[i, 1], ln_bias[i, 1])
    return x
```

```python
import functools

import jax
import jax.numpy as jnp
from jax import lax
from jax.experimental import pallas as pl
from jax.experimental.pallas import tpu as pltpu

GRID_W = 64
HEAD_DIM = 128
N_KV_HEADS = 2
GQA_GROUP = 4
N_HEADS = N_KV_HEADS * GQA_GROUP
ATTN_W = N_HEADS * HEAD_DIM
KV_W = N_KV_HEADS * HEAD_DIM
LRU_BLOCKS = 8
CONV_W = 4
RG_C = 8.0
ROPE_THETA = 10000.0
ROPE_AXIS_DIM = HEAD_DIM // 2
N_EXPERTS = 8
TOP_K = 2
LN_EPS = 1e-6

LANES = 128
SUBLANES = 8
VMEM_LIMIT = 56 * 1024 * 1024
MOE_ROWS = 512
F32 = jnp.float32
BF16 = jnp.bfloat16


def _params(sem, vmem=VMEM_LIMIT):
    return pltpu.CompilerParams(dimension_semantics=sem, vmem_limit_bytes=vmem)


def _const_spec(shape):
    nd = len(shape)
    return pl.BlockSpec(shape, lambda *_: (0,) * nd, pipeline_mode=pl.Buffered(1))


def _ln_stats(x):
    mu = jnp.mean(x, axis=-1, keepdims=True)
    xc = x - mu
    var = jnp.mean(xc * xc, axis=-1, keepdims=True)
    return xc * lax.rsqrt(var + LN_EPS)


def _rms(x, gain):
    return x * lax.rsqrt(jnp.mean(x * x, axis=-1, keepdims=True) + LN_EPS) * gain


def _mod_kernel(c_ref, w_ref, b_ref, o_ref):
    c = c_ref[...]
    s = (c * jax.nn.sigmoid(c)).astype(BF16)
    o_ref[0] = jnp.dot(s, w_ref[0].astype(BF16), preferred_element_type=F32) + b_ref[0]


def _modulation(cc, w_mod, b_mod):
    depth, d, n = w_mod.shape
    tn = 1024
    rows = cc.shape[0]
    return pl.pallas_call(
        _mod_kernel,
        grid=(depth, n // tn),
        in_specs=[
            pl.BlockSpec((rows, d), lambda l, j: (0, 0)),
            pl.BlockSpec((1, d, tn), lambda l, j: (l, 0, j)),
            pl.BlockSpec((1, 1, tn), lambda l, j: (l, 0, j)),
        ],
        out_specs=pl.BlockSpec((1, rows, tn), lambda l, j: (l, 0, j)),
        out_shape=jax.ShapeDtypeStruct((depth, rows, n), F32),
        compiler_params=_params(("arbitrary", "arbitrary")),
        name="modulation",
    )(cc, w_mod, b_mod.reshape(depth, 1, n))


def _inproj_kernel(x_ref, sh_ref, sc_ref, w_ref, qg_ref, kg_ref, cos_ref, sa_ref, sb_ref,
                   q_ref, k_ref, v_ref, u_ref, gb_ref):
    h = _ln_stats(x_ref[0]) * (1.0 + sc_ref[0]) + sh_ref[0]
    hb = h.astype(BF16)
    cos, sa, sb = cos_ref[...], sa_ref[...], sb_ref[...]

    def norm_rope(xh, gain):
        xn = _rms(xh, gain)
        half = ROPE_AXIS_DIM // 2
        return (xn * cos + pltpu.roll(xn, half, axis=1) * sa
                + pltpu.roll(xn, HEAD_DIM - half, axis=1) * sb)

    q = jnp.dot(hb, w_ref[:, 0:ATTN_W], preferred_element_type=F32)
    q_scale = HEAD_DIM ** -0.5
    for hd in range(N_HEADS):
        sl = slice(hd * HEAD_DIM, (hd + 1) * HEAD_DIM)
        q_ref[0, :, sl] = (norm_rope(q[:, sl], qg_ref[...]) * q_scale).astype(q_ref.dtype)
    k = jnp.dot(hb, w_ref[:, ATTN_W:ATTN_W + KV_W], preferred_element_type=F32)
    for hd in range(N_KV_HEADS):
        sl = slice(hd * HEAD_DIM, (hd + 1) * HEAD_DIM)
        k_ref[0, :, sl] = norm_rope(k[:, sl], kg_ref[...]).astype(k_ref.dtype)
    c0 = ATTN_W + KV_W
    v_ref[0] = jnp.dot(hb, w_ref[:, c0:c0 + KV_W], preferred_element_type=F32).astype(v_ref.dtype)
    c0 += KV_W
    lru_w = u_ref.shape[-1]
    u_ref[0] = jnp.dot(hb, w_ref[:, c0:c0 + lru_w], preferred_element_type=F32)
    c0 += lru_w
    gb_ref[0] = jnp.dot(hb, w_ref[:, c0:c0 + lru_w], preferred_element_type=F32)


def _in_projection(x, shift, scale, w_in, q_gain, k_gain, rope):
    b, n, d = x.shape
    lru_w = (w_in.shape[1] - ATTN_W - 2 * KV_W) // 2
    tm = min(n, 512)
    row = lambda bi, i: (bi, i, 0)
    vec = lambda bi, i: (bi, 0, 0)
    tab = pl.BlockSpec((tm, HEAD_DIM), lambda bi, i: (i, 0))
    return pl.pallas_call(
        _inproj_kernel,
        grid=(b, n // tm),
        in_specs=[
            pl.BlockSpec((1, tm, d), row),
            pl.BlockSpec((1, 1, d), vec),
            pl.BlockSpec((1, 1, d), vec),
            _const_spec(w_in.shape),
            _const_spec((1, HEAD_DIM)),
            _const_spec((1, HEAD_DIM)),
            tab, tab, tab,
        ],
        out_specs=[
            pl.BlockSpec((1, tm, ATTN_W), row),
            pl.BlockSpec((1, tm, KV_W), row),
            pl.BlockSpec((1, tm, KV_W), row),
            pl.BlockSpec((1, tm, lru_w), row),
            pl.BlockSpec((1, tm, lru_w), row),
        ],
        out_shape=[
            jax.ShapeDtypeStruct((b, n, ATTN_W), BF16),
            jax.ShapeDtypeStruct((b, n, KV_W), BF16),
            jax.ShapeDtypeStruct((b, n, KV_W), BF16),
            jax.ShapeDtypeStruct((b, n, lru_w), F32),
            jax.ShapeDtypeStruct((b, n, lru_w), F32),
        ],
        compiler_params=_params(("arbitrary", "arbitrary")),
        name="in_projection",
    )(x, shift, scale, w_in, q_gain.reshape(1, -1), k_gain.reshape(1, -1), *rope)


def _attn_kernel(*refs, n_src, tk):
    q_ref = refs[0]
    kv_refs = refs[1:1 + 2 * n_src]
    o_ref = refs[1 + 2 * n_src]
    m_ref, l_ref, acc_ref = refs[2 + 2 * n_src:]
    tq = q_ref.shape[1]
    q = jnp.concatenate(
        [q_ref[0, :, g * HEAD_DIM:(g + 1) * HEAD_DIM] for g in range(GQA_GROUP)], axis=0)
    m_ref[...] = jnp.full(m_ref.shape, -jnp.inf, F32)
    l_ref[...] = jnp.zeros(l_ref.shape, F32)
    acc_ref[...] = jnp.zeros(acc_ref.shape, F32)

    def update(k, v):
        s = lax.dot_general(q, k, (((1,), (1,)), ((), ())), preferred_element_type=F32)
        m_old = m_ref[...]
        m_new = jnp.maximum(m_old, jnp.max(s, axis=-1, keepdims=True))
        alpha = jnp.exp(m_old - m_new)
        p = jnp.exp(s - m_new)
        l_ref[...] = alpha * l_ref[...] + jnp.sum(p, axis=-1, keepdims=True)
        acc_ref[...] = alpha * acc_ref[...] + jnp.dot(p.astype(BF16), v, preferred_element_type=F32)
        m_ref[...] = m_new

    for si in range(n_src):
        k_ref, v_ref = kv_refs[2 * si], kv_refs[2 * si + 1]
        n_keys = k_ref.shape[1]
        chunk = min(tk, n_keys)

        def body(j, carry, k_ref=k_ref, v_ref=v_ref, chunk=chunk):
            off = pl.multiple_of(j * chunk, chunk)
            update(k_ref[0, pl.ds(off, chunk), :], v_ref[0, pl.ds(off, chunk), :])
            return carry

        lax.fori_loop(0, n_keys // chunk, body, 0)

    out = acc_ref[...] / l_ref[...]
    for g in range(GQA_GROUP):
        o_ref[0, :, g * HEAD_DIM:(g + 1) * HEAD_DIM] = out[g * tq:(g + 1) * tq].astype(o_ref.dtype)


def _attention(q, kv_sources):
    b, sq, _ = q.shape
    tq = min(sq, 256)
    gw = GQA_GROUP * HEAD_DIM
    in_specs = [pl.BlockSpec((1, tq, gw), lambda bi, g, i: (bi, i, g))]
    args = [q]
    for k, v in kv_sources:
        spec = pl.BlockSpec((1, k.shape[1], HEAD_DIM), lambda bi, g, i: (bi, 0, g))
        in_specs += [spec, spec]
        args += [k, v]
    rows = GQA_GROUP * tq
    return pl.pallas_call(
        functools.partial(_attn_kernel, n_src=len(kv_sources), tk=512),
        grid=(b, N_KV_HEADS, sq // tq),
        in_specs=in_specs,
        out_specs=pl.BlockSpec((1, tq, gw), lambda bi, g, i: (bi, i, g)),
        out_shape=jax.ShapeDtypeStruct((b, sq, ATTN_W), F32),
        scratch_shapes=[
            pltpu.VMEM((rows, 1), F32),
            pltpu.VMEM((rows, 1), F32),
            pltpu.VMEM((rows, HEAD_DIM), F32),
        ],
        compiler_params=_params(("arbitrary", "arbitrary", "arbitrary")),
        name="attention",
    )(*args)


def _lru_kernel(*refs, reverse, fuse, n_chunks):
    (u_ref, up_ref, un_ref, cw_ref, cb_ref, wg_ref, ba_ref, bx_ref, lam_ref, h0_ref) = refs[:10]
    if fuse:
        hf_ref, gb_ref, gain_ref, o_ref, ext_ref, a_ref, b_ref, carry_ref = refs[10:]
    else:
        o_ref, ext_ref, a_ref, b_ref, carry_ref = refs[10:]
    tc, width = u_ref.shape[1], u_ref.shape[2]
    bw = width // LRU_BLOCKS
    j = pl.program_id(1)
    cj = (n_chunks - 1 - j) if reverse else j

    ext_ref[0:SUBLANES, :] = jnp.where(cj == 0, 0.0, up_ref[0])
    ext_ref[SUBLANES:SUBLANES + tc, :] = u_ref[0]
    ext_ref[SUBLANES + tc:, :] = jnp.where(cj == n_chunks - 1, 0.0, un_ref[0])
    pad_l = (CONV_W - 1) // 2
    uc = cb_ref[...]
    for tap in range(CONV_W):
        uc = uc + ext_ref[pl.ds(SUBLANES - pad_l + tap, tc), :] * cw_ref[tap:tap + 1, :]

    for n in range(LRU_BLOCKS):
        sl = slice(n * bw, (n + 1) * bw)
        ub = uc[:, sl]
        g = jnp.dot(ub.astype(BF16), wg_ref[n], preferred_element_type=F32)
        r = jax.nn.sigmoid(g[:, :bw] + ba_ref[:, sl])
        gi = jax.nn.sigmoid(g[:, bw:] + bx_ref[:, sl])
        log_a = -RG_C * r * jax.nn.softplus(-lam_ref[:, sl])
        a = jnp.exp(log_a)
        a_ref[:, sl] = a
        b_ref[:, sl] = jnp.sqrt(1.0 - a * a) * gi * ub

    row = lax.broadcasted_iota(jnp.int32, (SUBLANES, width), 0)
    n_groups = tc // SUBLANES
    last = 0 if reverse else SUBLANES - 1

    def group(gidx, carry):
        gi_ = (n_groups - 1 - gidx) if reverse else gidx
        off = pl.multiple_of(gi_ * SUBLANES, SUBLANES)
        a = a_ref[pl.ds(off, SUBLANES), :]
        bb = b_ref[pl.ds(off, SUBLANES), :]
        for s in (1, 2, 4):
            if reverse:
                keep = row < SUBLANES - s
                shift = SUBLANES - s
            else:
                keep = row >= s
                shift = s
            a_sh = jnp.where(keep, pltpu.roll(a, shift, axis=0), 1.0)
            b_sh = jnp.where(keep, pltpu.roll(bb, shift, axis=0), 0.0)
            bb = a * b_sh + bb
            a = a * a_sh
        h = a * carry + bb
        b_ref[pl.ds(off, SUBLANES), :] = h
        return jnp.broadcast_to(h[last:last + 1, :], (SUBLANES, width))

    @pl.when(j == 0)
    def _():
        carry_ref[...] = jnp.broadcast_to(h0_ref[0], (SUBLANES, width))

    carry_ref[...] = lax.fori_loop(0, n_groups, group, carry_ref[...])

    if fuse:
        rec = (hf_ref[0] + b_ref[...]) * jax.nn.gelu(gb_ref[0])
        o_ref[0] = _rms(rec, gain_ref[...]).astype(o_ref.dtype)
    else:
        o_ref[0] = b_ref[...]


def _rglru(u, conv_w, conv_b, w_gate, ba, bx, lam, h0, reverse, fused=None):
    b, n, width = u.shape
    tc = min(n, 512)
    n_chunks = n // tc
    hb = tc // SUBLANES
    n_rows8 = n // SUBLANES

    def cidx(j):
        return (n_chunks - 1 - j) if reverse else j

    cur = pl.BlockSpec((1, tc, width), lambda bi, j: (bi, cidx(j), 0))
    prev = pl.BlockSpec((1, SUBLANES, width),
                        lambda bi, j: (bi, jnp.maximum(cidx(j) * hb - 1, 0), 0))
    nxt = pl.BlockSpec((1, SUBLANES, width),
                       lambda bi, j: (bi, jnp.minimum((cidx(j) + 1) * hb, n_rows8 - 1), 0))
    vec = _const_spec((1, width))
    in_specs = [cur, prev, nxt, _const_spec((CONV_W, width)), vec, _const_spec(w_gate.shape),
                vec, vec, vec, pl.BlockSpec((1, 1, width), lambda bi, j: (bi, 0, 0))]
    args = [u, u, u, conv_w, conv_b.reshape(1, -1), w_gate, ba.reshape(1, -1), bx.reshape(1, -1),
            lam.reshape(1, -1), h0]
    if fused is not None:
        in_specs += [cur, cur, vec]
        args += [fused[0], fused[1], fused[2].reshape(1, -1)]
    out_dtype = BF16 if fused is not None else F32
    return pl.pallas_call(
        functools.partial(_lru_kernel, reverse=reverse, fuse=fused is not None, n_chunks=n_chunks),
        grid=(b, n_chunks),
        in_specs=in_specs,
        out_specs=cur,
        out_shape=jax.ShapeDtypeStruct((b, n, width), out_dtype),
        scratch_shapes=[
            pltpu.VMEM((tc + 2 * SUBLANES, width), F32),
            pltpu.VMEM((tc, width), F32),
            pltpu.VMEM((tc, width), F32),
            pltpu.VMEM((SUBLANES, width), F32),
        ],
        compiler_params=_params(("arbitrary", "arbitrary")),
        name="rglru_bwd" if reverse else "rglru_fwd",
    )(*args)


def _outproj_kernel(att_ref, rec_ref, x_ref, w_ref, ga_ref, g1_ref, lg_ref, lb_ref, o_ref, *, alpha):
    an = _rms(att_ref[0], ga_ref[...]).astype(BF16)
    aw = att_ref.shape[-1]
    mix = jnp.dot(an, w_ref[0:aw, :], preferred_element_type=F32)
    mix = mix + jnp.dot(rec_ref[0], w_ref[aw:, :], preferred_element_type=F32)
    y = alpha * x_ref[0] + g1_ref[0] * mix
    o_ref[0] = _ln_stats(y) * lg_ref[...] + lb_ref[...]


def _out_projection(att, rec_n, x, w_out, gain_attn, g1, ln_g, ln_b, alpha):
    b, n, d = x.shape
    tm = min(n, 512)
    row = lambda bi, i: (bi, i, 0)
    return pl.pallas_call(
        functools.partial(_outproj_kernel, alpha=alpha),
        grid=(b, n // tm),
        in_specs=[
            pl.BlockSpec((1, tm, att.shape[-1]), row),
            pl.BlockSpec((1, tm, rec_n.shape[-1]), row),
            pl.BlockSpec((1, tm, d), row),
            _const_spec(w_out.shape),
            _const_spec((1, att.shape[-1])),
            pl.BlockSpec((1, 1, d), lambda bi, i: (bi, 0, 0)),
            _const_spec((1, d)),
            _const_spec((1, d)),
        ],
        out_specs=pl.BlockSpec((1, tm, d), row),
        out_shape=jax.ShapeDtypeStruct((b, n, d), F32),
        compiler_params=_params(("arbitrary", "arbitrary")),
        name="out_projection",
    )(att, rec_n, x, w_out, gain_attn.reshape(1, -1), g1, ln_g.reshape(1, -1), ln_b.reshape(1, -1))


def _ffn_kernel(x_ref, sh_ref, sc_ref, g2_ref, wg_ref, wu_ref, wd_ref, lg_ref, lb_ref, o_ref,
                h_ref, *, alpha):
    f = pl.program_id(2)

    @pl.when(f == 0)
    def _():
        h = _ln_stats(x_ref[0]) * (1.0 + sc_ref[0]) + sh_ref[0]
        h_ref[...] = h.astype(BF16)
        o_ref[0] = jnp.zeros(o_ref.shape[1:], F32)

    hb = h_ref[...]
    g = jnp.dot(hb, wg_ref[...], preferred_element_type=F32)
    u = jnp.dot(hb, wu_ref[...], preferred_element_type=F32)
    act = (g * jax.nn.sigmoid(g) * u).astype(BF16)
    o_ref[0] += jnp.dot(act, wd_ref[...], preferred_element_type=F32)

    @pl.when(f == pl.num_programs(2) - 1)
    def _():
        y = alpha * x_ref[0] + g2_ref[0] * o_ref[0]
        o_ref[0] = _ln_stats(y) * lg_ref[...] + lb_ref[...]


def _dense_ffn(x, shift, scale, g2, wg, wu, wd, ln_g, ln_b, alpha):
    b, n, d = x.shape
    ff = wg.shape[1]
    tm = min(n, 512)
    tf = 512
    row = lambda bi, i, f: (bi, i, 0)
    vec = pl.BlockSpec((1, 1, d), lambda bi, i, f: (bi, 0, 0))
    return pl.pallas_call(
        functools.partial(_ffn_kernel, alpha=alpha),
        grid=(b, n // tm, ff // tf),
        in_specs=[
            pl.BlockSpec((1, tm, d), row), vec, vec, vec,
            pl.BlockSpec((d, tf), lambda bi, i, f: (0, f)),
            pl.BlockSpec((d, tf), lambda bi, i, f: (0, f)),
            pl.BlockSpec((tf, d), lambda bi, i, f: (f, 0)),
            _const_spec((1, d)),
            _const_spec((1, d)),
        ],
        out_specs=pl.BlockSpec((1, tm, d), row),
        out_shape=jax.ShapeDtypeStruct((b, n, d), F32),
        scratch_shapes=[pltpu.VMEM((tm, d), BF16)],
        compiler_params=_params(("arbitrary", "arbitrary", "arbitrary")),
        name="dense_ffn",
    )(x, shift, scale, g2, wg, wu, wd, ln_g.reshape(1, -1), ln_b.reshape(1, -1))


def _router_kernel(x_ref, sh_ref, sc_ref, whi_ref, wlo_ref, h_ref, idx_ref, gate_ref):
    h = _ln_stats(x_ref[0]) * (1.0 + sc_ref[0]) + sh_ref[0]
    h_ref[0] = h
    hi = h.astype(BF16)
    lo = (h - hi.astype(F32)).astype(BF16)
    nt = (((1,), (1,)), ((), ()))
    logits = (lax.dot_general(whi_ref[...], hi, nt, preferred_element_type=F32)
              + lax.dot_general(whi_ref[...], lo, nt, preferred_element_type=F32)
              + lax.dot_general(wlo_ref[...], hi, nt, preferred_element_type=F32))
    e_iota = lax.broadcasted_iota(jnp.int32, logits.shape, 0)
    m1 = jnp.max(logits, axis=0, keepdims=True)
    i1 = jnp.min(jnp.where(logits == m1, e_iota, N_EXPERTS), axis=0, keepdims=True)
    rest = jnp.where(e_iota == i1, -jnp.inf, logits)
    m2 = jnp.max(rest, axis=0, keepdims=True)
    i2 = jnp.min(jnp.where(rest == m2, e_iota, N_EXPERTS), axis=0, keepdims=True)
    e2 = jnp.exp(m2 - m1)
    den = 1.0 + e2
    idx_ref[...] = jnp.where(e_iota == 0, i1, jnp.where(e_iota == 1, i2, 0))
    gate_ref[...] = jnp.where(e_iota == 0, 1.0 / den, jnp.where(e_iota == 1, e2 / den, 0.0))


def _router(x, shift, scale, w_router):
    b, n, d = x.shape
    tm = min(n, 512)
    nb = n // tm
    wt = w_router.T
    w_hi = wt.astype(BF16)
    w_lo = (wt - w_hi.astype(F32)).astype(BF16)
    vec = pl.BlockSpec((1, 1, d), lambda bi, i: (bi, 0, 0))
    col = pl.BlockSpec((N_EXPERTS, tm), lambda bi, i: (0, bi * nb + i))
    return pl.pallas_call(
        _router_kernel,
        grid=(b, nb),
        in_specs=[pl.BlockSpec((1, tm, d), lambda bi, i: (bi, i, 0)), vec, vec,
                  _const_spec(w_hi.shape), _const_spec(w_lo.shape)],
        out_specs=[pl.BlockSpec((1, tm, d), lambda bi, i: (bi, i, 0)), col, col],
        out_shape=[
            jax.ShapeDtypeStruct((b, n, d), F32),
            jax.ShapeDtypeStruct((N_EXPERTS, b * n), jnp.int32),
            jax.ShapeDtypeStruct((N_EXPERTS, b * n), F32),
        ],
        compiler_params=_params(("arbitrary", "arbitrary")),
        name="router",
    )(x, shift, scale, w_hi, w_lo)


def _gather_kernel(src_ref, used_ref, tok_ref, o_ref, sem):
    blk = pl.program_id(0)
    rows = o_ref.shape[0]

    def row_copy(r):
        t = src_ref[blk * rows + r]
        return pltpu.make_async_copy(tok_ref.at[pl.ds(t, 1), :], o_ref.at[pl.ds(r, 1), :], sem)

    @pl.when(blk < used_ref[0])
    def _():
        def start(r, c):
            row_copy(r).start()
            return c

        def wait(r, c):
            row_copy(r).wait()
            return c

        lax.fori_loop(0, rows, start, 0)
        lax.fori_loop(0, rows, wait, 0)

    @pl.when(blk >= used_ref[0])
    def _():
        o_ref[...] = jnp.zeros(o_ref.shape, o_ref.dtype)


def _dispatch_gather(tok, src, n_used, n_blocks):
    n, d = tok.shape
    rows = MOE_ROWS
    return pl.pallas_call(
        _gather_kernel,
        grid_spec=pltpu.PrefetchScalarGridSpec(
            num_scalar_prefetch=2,
            grid=(n_blocks,),
            in_specs=[pl.BlockSpec(memory_space=pl.ANY)],
            out_specs=pl.BlockSpec((rows, d), lambda blk, src, used: (blk, 0)),
            scratch_shapes=[pltpu.SemaphoreType.DMA(())],
        ),
        out_shape=jax.ShapeDtypeStruct((n_blocks * rows, d), tok.dtype),
        compiler_params=_params(("arbitrary",)),
        name="dispatch_gather",
    )(src, n_used, tok)


def _expert_kernel(be_ref, used_ref, x_ref, wg_ref, wu_ref, wd_ref, o_ref, h_ref):
    blk, f = pl.program_id(0), pl.program_id(1)

    @pl.when(blk < used_ref[0])
    def _():
        @pl.when(f == 0)
        def _():
            h_ref[...] = x_ref[...].astype(BF16)
            o_ref[...] = jnp.zeros(o_ref.shape, F32)

        hb = h_ref[...]
        g = jnp.dot(hb, wg_ref[0], preferred_element_type=F32)
        u = jnp.dot(hb, wu_ref[0], preferred_element_type=F32)
        act = (g * jax.nn.sigmoid(g) * u).astype(BF16)
        o_ref[...] += jnp.dot(act, wd_ref[0], preferred_element_type=F32)

    @pl.when(jnp.logical_and(blk >= used_ref[0], f == 0))
    def _():
        o_ref[...] = jnp.zeros(o_ref.shape, F32)


def _expert_ffn(xs, block_expert, n_used, wg, wu, wd):
    n_slots, d = xs.shape
    rows = MOE_ROWS
    n_blocks = n_slots // rows
    ff = wg.shape[-1]
    tf = 512
    nf = ff // tf

    def blk_of(blk, used):
        return jnp.minimum(blk, used[0] - 1)

    def f_of(blk, f, used):
        return jnp.where(blk < used[0], f, nf - 1)

    return pl.pallas_call(
        _expert_kernel,
        grid_spec=pltpu.PrefetchScalarGridSpec(
            num_scalar_prefetch=2,
            grid=(n_blocks, nf),
            in_specs=[
                pl.BlockSpec((rows, d), lambda blk, f, be, used: (blk_of(blk, used), 0)),
                pl.BlockSpec((1, d, tf), lambda blk, f, be, used: (be[blk_of(blk, used)], 0, f_of(blk, f, used))),
                pl.BlockSpec((1, d, tf), lambda blk, f, be, used: (be[blk_of(blk, used)], 0, f_of(blk, f, used))),
                pl.BlockSpec((1, tf, d), lambda blk, f, be, used: (be[blk_of(blk, used)], f_of(blk, f, used), 0)),
            ],
            out_specs=pl.BlockSpec((rows, d), lambda blk, f, be, used: (blk, 0)),
            scratch_shapes=[pltpu.VMEM((rows, d), BF16)],
        ),
        out_shape=jax.ShapeDtypeStruct((n_slots, d), F32),
        compiler_params=_params(("arbitrary", "arbitrary")),
        name="expert_ffn",
    )(block_expert, n_used, xs, wg, wu, wd)


def _combine_kernel(dest_ref, eo_ref, gate_ref, x_ref, g2_ref, lg_ref, lb_ref, o_ref, buf_ref, sem,
                    *, alpha, blocks_per_batch):
    tm = x_ref.shape[1]
    base = (pl.program_id(0) * blocks_per_batch + pl.program_id(1)) * tm

    def row_copy(r, k):
        slot = dest_ref[(base + r) * TOP_K + k]
        return pltpu.make_async_copy(eo_ref.at[pl.ds(slot, 1), :], buf_ref.at[k, pl.ds(r, 1), :], sem)

    def start(r, c):
        for k in range(TOP_K):
            row_copy(r, k).start()
        return c

    def wait(r, c):
        for k in range(TOP_K):
            row_copy(r, k).wait()
        return c

    lax.fori_loop(0, tm, start, 0)
    lax.fori_loop(0, tm, wait, 0)
    gates = gate_ref[...]
    y = gates[:, 0:1] * buf_ref[0]
    for k in range(1, TOP_K):
        y = y + gates[:, k:k + 1] * buf_ref[k]
    z = alpha * x_ref[0] + g2_ref[0] * y
    o_ref[0] = _ln_stats(z) * lg_ref[...] + lb_ref[...]


def _expert_combine(eo, dest, gates, x, g2, ln_g, ln_b, alpha):
    b, n, d = x.shape
    tm = min(n, 256)
    nb = n // tm
    return pl.pallas_call(
        functools.partial(_combine_kernel, alpha=alpha, blocks_per_batch=nb),
        grid_spec=pltpu.PrefetchScalarGridSpec(
            num_scalar_prefetch=1,
            grid=(b, nb),
            in_specs=[
                pl.BlockSpec(memory_space=pl.ANY),
                pl.BlockSpec((tm, TOP_K), lambda bi, i, dest: (bi * nb + i, 0)),
                pl.BlockSpec((1, tm, d), lambda bi, i, dest: (bi, i, 0)),
                pl.BlockSpec((1, 1, d), lambda bi, i, dest: (bi, 0, 0)),
                pl.BlockSpec((1, d), lambda bi, i, dest: (0, 0)),
                pl.BlockSpec((1, d), lambda bi, i, dest: (0, 0)),
            ],
            out_specs=pl.BlockSpec((1, tm, d), lambda bi, i, dest: (bi, i, 0)),
            scratch_shapes=[pltpu.VMEM((TOP_K, tm, d), F32), pltpu.SemaphoreType.DMA(())],
        ),
        out_shape=jax.ShapeDtypeStruct((b, n, d), F32),
        compiler_params=_params(("arbitrary", "arbitrary")),
        name="expert_combine",
    )(dest, eo, gates, x, g2, ln_g.reshape(1, -1), ln_b.reshape(1, -1))


def _moe(x, shift, scale, g2, w_router, wg, wu, wd, ln_g, ln_b, alpha):
    b, n, d = x.shape
    n_tok = b * n
    h, ridx, rgate = _router(x, shift, scale, w_router)
    exp_a = ridx[:TOP_K].T.reshape(-1)
    gates = rgate[:TOP_K].T
    onehot = (exp_a[:, None] == jnp.arange(N_EXPERTS, dtype=jnp.int32)).astype(jnp.int32)
    csum = jnp.cumsum(onehot, axis=0)
    rank = jnp.take_along_axis(csum, exp_a[:, None], axis=1)[:, 0] - 1
    counts = csum[-1]
    padded = (counts + MOE_ROWS - 1) // MOE_ROWS * MOE_ROWS
    pad_ends = jnp.cumsum(padded)
    pad_starts = pad_ends - padded
    dest = (pad_starts[exp_a] + rank).astype(jnp.int32)
    n_blocks = (n_tok * TOP_K) // MOE_ROWS + N_EXPERTS
    tok_a = jnp.repeat(jnp.arange(n_tok, dtype=jnp.int32), TOP_K)
    src = jnp.zeros((n_blocks * MOE_ROWS,), jnp.int32).at[dest].set(tok_a)
    block_start = jnp.arange(n_blocks, dtype=jnp.int32) * MOE_ROWS
    block_expert = jnp.minimum(jnp.searchsorted(pad_ends, block_start, side='right'),
                               N_EXPERTS - 1).astype(jnp.int32)
    n_used = (pad_ends[-1:] // MOE_ROWS).astype(jnp.int32)

    xs = _dispatch_gather(h.reshape(n_tok, d), src, n_used, n_blocks)
    eo = _expert_ffn(xs, block_expert, n_used, wg, wu, wd)
    return _expert_combine(eo, dest, gates, x, g2, ln_g, ln_b, alpha)


def _rope_tables(n_tokens):
    half = ROPE_AXIS_DIM // 2
    t = jnp.arange(n_tokens, dtype=jnp.int32)
    pos = jnp.stack([(t // GRID_W).astype(F32), (t % GRID_W).astype(F32)], axis=1)
    inv_freq = ROPE_THETA ** (-jnp.arange(half, dtype=F32) / half)
    ang = pos[:, :, None] * inv_freq
    cos = jnp.cos(ang)
    sin = jnp.sin(ang)
    zero = jnp.zeros_like(sin)
    cos_t = jnp.stack([cos, cos], axis=2).reshape(n_tokens, HEAD_DIM)
    sa = jnp.stack([zero, sin], axis=2).reshape(n_tokens, HEAD_DIM)
    sb = jnp.stack([-sin, zero], axis=2).reshape(n_tokens, HEAD_DIM)
    return cos_t, sa, sb


def _identity_rope(n_tokens):
    one = jnp.ones((n_tokens, HEAD_DIM), F32)
    zero = jnp.zeros((n_tokens, HEAD_DIM), F32)
    return one, zero, zero


def kernel(x, c, ctx, c_ctx, w_mod, b_mod, w_in, q_gain, k_gain, conv_w, conv_b, lru_wa, lru_ba, lru_wx, lru_bx, lru_lambda, out_gain_attn, out_gain_lru, w_out, ln_gain, ln_bias, ffn_w_gate, ffn_w_up, ffn_w_down, router_w, exp_w_gate, exp_w_up, exp_w_down):
    b, s, d = x.shape
    n_ctx = ctx.shape[1]
    depth = w_mod.shape[0]
    lru_w = lru_ba.shape[-1]
    alpha = (2 * depth) ** 0.25

    rope_lat = _rope_tables(s)
    rope_ctx = _identity_rope(n_ctx)

    cc = jnp.concatenate([c, c_ctx[None, :], jnp.zeros((SUBLANES - b - 1, d), F32)], axis=0)
    mod = _modulation(cc, w_mod, b_mod)

    w_in_b = w_in.astype(BF16)
    w_out_b = w_out.astype(BF16)
    w_gates = jnp.concatenate([lru_wa, lru_wx], axis=-1).astype(BF16)
    zero_state = jnp.zeros((b, 1, lru_w), F32)

    for i in range(depth):
        last = i == depth - 1
        lat = [m[:, None, :] for m in jnp.split(mod[i, :b], 6, axis=-1)]
        cmod = [jnp.broadcast_to(m[None, None, :], (b, 1, d)) for m in jnp.split(mod[i, b], 6)]
        sh1, sc1, g1, sh2, sc2, g2 = lat
        csh1, csc1, cg1, csh2, csc2, cg2 = cmod

        q_l, k_l, v_l, u_l, gb_l = _in_projection(x, sh1, sc1, w_in_b[i], q_gain[i], k_gain[i], rope_lat)
        q_c, k_c, v_c, u_c, gb_c = _in_projection(ctx, csh1, csc1, w_in_b[i], q_gain[i], k_gain[i], rope_ctx)

        att_l = _attention(q_l, [(k_c, v_c), (k_l, v_l)])

        lru = lambda u, dr, h0, rev, fused=None: _rglru(
            u, conv_w[i], conv_b[i], w_gates[i, dr], lru_ba[i, dr], lru_bx[i, dr], lru_lambda[i, dr],
            h0, rev, fused)
        hc_f = lru(u_c, 0, zero_state, False)
        hl_f = lru(u_l, 0, hc_f[:, -1:, :], False)
        hc_b = lru(u_c, 1, zero_state, True)
        rec_l = lru(u_l, 1, hc_b[:, 0:1, :], True, (hl_f, gb_l, out_gain_lru[i]))
        if not last:
            rec_c = lru(u_c, 1, zero_state, True, (hc_f, gb_c, out_gain_lru[i]))

        x_mid = _out_projection(att_l, rec_l, x, w_out_b[i], out_gain_attn[i], g1,
                                ln_gain[i, 0], ln_bias[i, 0], alpha)
        if not last:
            att_c = _attention(q_c, [(k_c, v_c)])
            ctx_mid = _out_projection(att_c, rec_c, ctx, w_out_b[i], out_gain_attn[i], cg1,
                                      ln_gain[i, 0], ln_bias[i, 0], alpha)

        j = i // 2
        if i % 2 == 0:
            wg, wu, wd = ffn_w_gate[j].astype(BF16), ffn_w_up[j].astype(BF16), ffn_w_down[j].astype(BF16)
            x = _dense_ffn(x_mid, sh2, sc2, g2, wg, wu, wd, ln_gain[i, 1], ln_bias[i, 1], alpha)
            if not last:
                ctx = _dense_ffn(ctx_mid, csh2, csc2, cg2, wg, wu, wd, ln_gain[i, 1], ln_bias[i, 1], alpha)
        else:
            wg, wu, wd = exp_w_gate[j].astype(BF16), exp_w_up[j].astype(BF16), exp_w_down[j].astype(BF16)
            x = _moe(x_mid, sh2, sc2, g2, router_w[j], wg, wu, wd, ln_gain[i, 1], ln_bias[i, 1], alpha)
            if not last:
                ctx = _moe(ctx_mid, csh2, csc2, cg2, router_w[j], wg, wu, wd,
                           ln_gain[i, 1], ln_bias[i, 1], alpha)
    return x
```

```python
import functools

import jax
import jax.numpy as jnp
from jax import lax
from jax.experimental import pallas as pl
from jax.experimental.pallas import tpu as pltpu

GRID_W = 64
HEAD_DIM = 128
N_KV_HEADS = 2
GQA_GROUP = 4
N_HEADS = N_KV_HEADS * GQA_GROUP
ATTN_W = N_HEADS * HEAD_DIM
KV_W = N_KV_HEADS * HEAD_DIM
LRU_BLOCKS = 8
CONV_W = 4
RG_C = 8.0
ROPE_THETA = 10000.0
ROPE_AXIS_DIM = HEAD_DIM // 2
N_EXPERTS = 8
TOP_K = 2
LN_EPS = 1e-6
LOG2_E = 1.4426950408889634

LANES = 128
SUBLANES = 8
VMEM_LIMIT = 56 * 1024 * 1024
MOE_ROWS = 512
ATTN_TQ = 256
ATTN_TK = 512
F32 = jnp.float32
BF16 = jnp.bfloat16


def _params(sem, vmem=VMEM_LIMIT):
    return pltpu.CompilerParams(dimension_semantics=sem, vmem_limit_bytes=vmem)


def _const_spec(shape):
    nd = len(shape)
    return pl.BlockSpec(shape, lambda *_: (0,) * nd, pipeline_mode=pl.Buffered(1))


def _ln_stats(x):
    mu = jnp.mean(x, axis=-1, keepdims=True)
    xc = x - mu
    var = jnp.mean(xc * xc, axis=-1, keepdims=True)
    return xc * lax.rsqrt(var + LN_EPS)


def _rms(x, gain):
    return x * lax.rsqrt(jnp.mean(x * x, axis=-1, keepdims=True) + LN_EPS) * gain


def _sigmoid(x):
    return 0.5 * jnp.tanh(0.5 * x) + 0.5


def _mod_kernel(c_ref, w_ref, b_ref, o_ref):
    c = c_ref[...]
    s = (c * _sigmoid(c)).astype(BF16)
    o_ref[0] = jnp.dot(s, w_ref[0].astype(BF16), preferred_element_type=F32) + b_ref[0]


def _modulation(cc, w_mod, b_mod):
    depth, d, n = w_mod.shape
    tn = 1024
    rows = cc.shape[0]
    return pl.pallas_call(
        _mod_kernel,
        grid=(depth, n // tn),
        in_specs=[
            pl.BlockSpec((rows, d), lambda l, j: (0, 0)),
            pl.BlockSpec((1, d, tn), lambda l, j: (l, 0, j)),
            pl.BlockSpec((1, 1, tn), lambda l, j: (l, 0, j)),
        ],
        out_specs=pl.BlockSpec((1, rows, tn), lambda l, j: (l, 0, j)),
        out_shape=jax.ShapeDtypeStruct((depth, rows, n), F32),
        compiler_params=_params(("arbitrary", "arbitrary")),
        name="modulation",
    )(cc, w_mod, b_mod.reshape(depth, 1, n))


def _inproj_kernel(x_ref, sh_ref, sc_ref, w_ref, qg_ref, kg_ref, cos_ref, sa_ref, sb_ref,
                   q_ref, k_ref, v_ref, u_ref, gb_ref):
    h = _ln_stats(x_ref[0]) * (1.0 + sc_ref[0]) + sh_ref[0]
    hb = h.astype(BF16)
    cos, sa, sb = cos_ref[...], sa_ref[...], sb_ref[...]

    def norm_rope(xh, gain):
        xn = _rms(xh, gain)
        half = ROPE_AXIS_DIM // 2
        return (xn * cos + pltpu.roll(xn, half, axis=1) * sa
                + pltpu.roll(xn, HEAD_DIM - half, axis=1) * sb)

    q = jnp.dot(hb, w_ref[:, 0:ATTN_W], preferred_element_type=F32)
    q_scale = HEAD_DIM ** -0.5 * LOG2_E
    for hd in range(N_HEADS):
        sl = slice(hd * HEAD_DIM, (hd + 1) * HEAD_DIM)
        q_ref[0, :, sl] = (norm_rope(q[:, sl], qg_ref[...]) * q_scale).astype(q_ref.dtype)
    k = jnp.dot(hb, w_ref[:, ATTN_W:ATTN_W + KV_W], preferred_element_type=F32)
    for hd in range(N_KV_HEADS):
        sl = slice(hd * HEAD_DIM, (hd + 1) * HEAD_DIM)
        k_ref[0, :, sl] = norm_rope(k[:, sl], kg_ref[...]).astype(k_ref.dtype)
    c0 = ATTN_W + KV_W
    v = jnp.dot(hb, w_ref[:, c0:c0 + KV_W], preferred_element_type=F32)
    for hd in range(N_KV_HEADS):
        v_ref[0, :, 2 * hd * HEAD_DIM:(2 * hd + 1) * HEAD_DIM] = (
            v[:, hd * HEAD_DIM:(hd + 1) * HEAD_DIM].astype(v_ref.dtype))
        v_ref[0, :, (2 * hd + 1) * HEAD_DIM:(2 * hd + 2) * HEAD_DIM] = jnp.ones(
            (v.shape[0], HEAD_DIM), v_ref.dtype)
    c0 += KV_W
    lru_w = u_ref.shape[-1]
    u_ref[0] = jnp.dot(hb, w_ref[:, c0:c0 + lru_w], preferred_element_type=F32)
    c0 += lru_w
    gb_ref[0] = jnp.dot(hb, w_ref[:, c0:c0 + lru_w], preferred_element_type=F32)


def _in_projection(x, shift, scale, w_in, q_gain, k_gain, rope):
    b, n, d = x.shape
    lru_w = (w_in.shape[1] - ATTN_W - 2 * KV_W) // 2
    tm = min(n, 512)
    row = lambda bi, i: (bi, i, 0)
    vec = lambda bi, i: (bi, 0, 0)
    tab = pl.BlockSpec((tm, HEAD_DIM), lambda bi, i: (i, 0))
    return pl.pallas_call(
        _inproj_kernel,
        grid=(b, n // tm),
        in_specs=[
            pl.BlockSpec((1, tm, d), row),
            pl.BlockSpec((1, 1, d), vec),
            pl.BlockSpec((1, 1, d), vec),
            _const_spec(w_in.shape),
            _const_spec((1, HEAD_DIM)),
            _const_spec((1, HEAD_DIM)),
            tab, tab, tab,
        ],
        out_specs=[
            pl.BlockSpec((1, tm, ATTN_W), row),
            pl.BlockSpec((1, tm, KV_W), row),
            pl.BlockSpec((1, tm, 2 * KV_W), row),
            pl.BlockSpec((1, tm, lru_w), row),
            pl.BlockSpec((1, tm, lru_w), row),
        ],
        out_shape=[
            jax.ShapeDtypeStruct((b, n, ATTN_W), BF16),
            jax.ShapeDtypeStruct((b, n, KV_W), BF16),
            jax.ShapeDtypeStruct((b, n, 2 * KV_W), BF16),
            jax.ShapeDtypeStruct((b, n, lru_w), F32),
            jax.ShapeDtypeStruct((b, n, lru_w), F32),
        ],
        compiler_params=_params(("arbitrary", "arbitrary")),
        name="in_projection",
    )(x, shift, scale, w_in, q_gain.reshape(1, -1), k_gain.reshape(1, -1), *rope)


def _attn_kernel(*refs, n_src, tk):
    q_ref = refs[0]
    kv_refs = refs[1:1 + 2 * n_src]
    o_ref = refs[1 + 2 * n_src]
    m_ref, acc_ref = refs[2 + 2 * n_src:]
    tq = q_ref.shape[1]
    q = jnp.concatenate(
        [q_ref[0, :, g * HEAD_DIM:(g + 1) * HEAD_DIM] for g in range(GQA_GROUP)], axis=0)
    m_ref[...] = jnp.full(m_ref.shape, -jnp.inf, F32)
    acc_ref[...] = jnp.zeros(acc_ref.shape, F32)

    def update(k, v):
        s = lax.dot_general(q, k, (((1,), (1,)), ((), ())), preferred_element_type=F32)
        m_old = m_ref[...]
        m_new = jnp.maximum(m_old, jnp.max(s, axis=-1, keepdims=True))
        alpha = jnp.exp2(m_old - m_new)
        p = jnp.exp2(s - jnp.concatenate([m_new] * (s.shape[1] // LANES), axis=1))
        pv = jnp.dot(p.astype(BF16), v, preferred_element_type=F32)
        acc_ref[...] = jnp.concatenate([alpha, alpha], axis=1) * acc_ref[...] + pv
        m_ref[...] = m_new

    for si in range(n_src):
        k_ref, v_ref = kv_refs[2 * si], kv_refs[2 * si + 1]
        n_keys = k_ref.shape[1]
        chunk = min(tk, n_keys)

        def body(j, carry, k_ref=k_ref, v_ref=v_ref, chunk=chunk):
            off = pl.multiple_of(j * chunk, chunk)
            update(k_ref[0, pl.ds(off, chunk), :], v_ref[0, pl.ds(off, chunk), :])
            return carry

        n_chunks = n_keys // chunk
        lax.fori_loop(0, n_chunks, body, 0, unroll=8 if n_chunks % 8 == 0 else 1)

    out = acc_ref[:, 0:HEAD_DIM] / acc_ref[:, HEAD_DIM:2 * HEAD_DIM]
    for g in range(GQA_GROUP):
        o_ref[0, :, g * HEAD_DIM:(g + 1) * HEAD_DIM] = out[g * tq:(g + 1) * tq].astype(o_ref.dtype)


def _attention(q, kv_sources):
    b, sq, _ = q.shape
    tq = min(sq, ATTN_TQ)
    gw = GQA_GROUP * HEAD_DIM
    in_specs = [pl.BlockSpec((1, tq, gw), lambda bi, g, i: (bi, i, g))]
    args = [q]
    for k, v in kv_sources:
        in_specs += [pl.BlockSpec((1, k.shape[1], HEAD_DIM), lambda bi, g, i: (bi, 0, g)),
                     pl.BlockSpec((1, k.shape[1], 2 * HEAD_DIM), lambda bi, g, i: (bi, 0, g))]
        args += [k, v]
    rows = GQA_GROUP * tq
    return pl.pallas_call(
        functools.partial(_attn_kernel, n_src=len(kv_sources), tk=ATTN_TK),
        grid=(b, N_KV_HEADS, sq // tq),
        in_specs=in_specs,
        out_specs=pl.BlockSpec((1, tq, gw), lambda bi, g, i: (bi, i, g)),
        out_shape=jax.ShapeDtypeStruct((b, sq, ATTN_W), F32),
        scratch_shapes=[
            pltpu.VMEM((rows, LANES), F32),
            pltpu.VMEM((rows, 2 * HEAD_DIM), F32),
        ],
        compiler_params=_params(("arbitrary", "arbitrary", "arbitrary")),
        name="attention",
    )(*args)


def _lru_kernel(*refs, reverse, fuse, n_chunks):
    (u_ref, up_ref, un_ref, cw_ref, cb_ref, wg_ref, ba_ref, bx_ref, lam_ref, h0_ref) = refs[:10]
    if fuse:
        hf_ref, gb_ref, gain_ref, o_ref, ext_ref, a_ref, b_ref, carry_ref = refs[10:]
    else:
        o_ref, ext_ref, a_ref, b_ref, carry_ref = refs[10:]
    tc, width = u_ref.shape[1], u_ref.shape[2]
    bw = width // LRU_BLOCKS
    j = pl.program_id(1)
    cj = (n_chunks - 1 - j) if reverse else j

    ext_ref[0:SUBLANES, :] = jnp.where(cj == 0, 0.0, up_ref[0])
    ext_ref[SUBLANES:SUBLANES + tc, :] = u_ref[0]
    ext_ref[SUBLANES + tc:, :] = jnp.where(cj == n_chunks - 1, 0.0, un_ref[0])
    pad_l = (CONV_W - 1) // 2
    uc = cb_ref[...]
    for tap in range(CONV_W):
        uc = uc + ext_ref[pl.ds(SUBLANES - pad_l + tap, tc), :] * cw_ref[tap:tap + 1, :]

    for n in range(LRU_BLOCKS):
        sl = slice(n * bw, (n + 1) * bw)
        ub = uc[:, sl]
        g = jnp.dot(ub.astype(BF16), wg_ref[n], preferred_element_type=F32)
        r = _sigmoid(g[:, :bw] + ba_ref[:, sl])
        gi = _sigmoid(g[:, bw:] + bx_ref[:, sl])
        log_a = -RG_C * r * jax.nn.softplus(-lam_ref[:, sl])
        a = jnp.exp(log_a)
        a_ref[:, sl] = a
        x2 = 1.0 - a * a
        b_ref[:, sl] = x2 * lax.rsqrt(jnp.maximum(x2, 1e-30)) * gi * ub

    row = lax.broadcasted_iota(jnp.int32, (SUBLANES, width), 0)
    n_groups = tc // SUBLANES
    last = 0 if reverse else SUBLANES - 1

    def group(gidx, carry):
        gi_ = (n_groups - 1 - gidx) if reverse else gidx
        off = pl.multiple_of(gi_ * SUBLANES, SUBLANES)
        a = a_ref[pl.ds(off, SUBLANES), :]
        bb = b_ref[pl.ds(off, SUBLANES), :]
        for s in (1, 2, 4):
            if reverse:
                keep = row < SUBLANES - s
                shift = SUBLANES - s
            else:
                keep = row >= s
                shift = s
            a_sh = jnp.where(keep, pltpu.roll(a, shift, axis=0), 1.0)
            b_sh = jnp.where(keep, pltpu.roll(bb, shift, axis=0), 0.0)
            bb = a * b_sh + bb
            a = a * a_sh
        h = a * carry + bb
        b_ref[pl.ds(off, SUBLANES), :] = h
        return jnp.broadcast_to(h[last:last + 1, :], (SUBLANES, width))

    @pl.when(j == 0)
    def _():
        carry_ref[...] = jnp.broadcast_to(h0_ref[0], (SUBLANES, width))

    carry_ref[...] = lax.fori_loop(0, n_groups, group, carry_ref[...])

    if fuse:
        rec = (hf_ref[0] + b_ref[...]) * jax.nn.gelu(gb_ref[0])
        o_ref[0] = _rms(rec, gain_ref[...]).astype(o_ref.dtype)
    else:
        o_ref[0] = b_ref[...]


def _rglru(u, conv_w, conv_b, w_gate, ba, bx, lam, h0, reverse, fused=None):
    b, n, width = u.shape
    tc = min(n, 512)
    n_chunks = n // tc
    hb = tc // SUBLANES
    n_rows8 = n // SUBLANES

    def cidx(j):
        return (n_chunks - 1 - j) if reverse else j

    cur = pl.BlockSpec((1, tc, width), lambda bi, j: (bi, cidx(j), 0))
    prev = pl.BlockSpec((1, SUBLANES, width),
                        lambda bi, j: (bi, jnp.maximum(cidx(j) * hb - 1, 0), 0))
    nxt = pl.BlockSpec((1, SUBLANES, width),
                       lambda bi, j: (bi, jnp.minimum((cidx(j) + 1) * hb, n_rows8 - 1), 0))
    vec = _const_spec((1, width))
    in_specs = [cur, prev, nxt, _const_spec((CONV_W, width)), vec, _const_spec(w_gate.shape),
                vec, vec, vec, pl.BlockSpec((1, 1, width), lambda bi, j: (bi, 0, 0))]
    args = [u, u, u, conv_w, conv_b.reshape(1, -1), w_gate, ba.reshape(1, -1), bx.reshape(1, -1),
            lam.reshape(1, -1), h0]
    if fused is not None:
        in_specs += [cur, cur, vec]
        args += [fused[0], fused[1], fused[2].reshape(1, -1)]
    out_dtype = BF16 if fused is not None else F32
    return pl.pallas_call(
        functools.partial(_lru_kernel, reverse=reverse, fuse=fused is not None, n_chunks=n_chunks),
        grid=(b, n_chunks),
        in_specs=in_specs,
        out_specs=cur,
        out_shape=jax.ShapeDtypeStruct((b, n, width), out_dtype),
        scratch_shapes=[
            pltpu.VMEM((tc + 2 * SUBLANES, width), F32),
            pltpu.VMEM((tc, width), F32),
            pltpu.VMEM((tc, width), F32),
            pltpu.VMEM((SUBLANES, width), F32),
        ],
        compiler_params=_params(("arbitrary", "arbitrary")),
        name="rglru_bwd" if reverse else "rglru_fwd",
    )(*args)


def _outproj_kernel(att_ref, rec_ref, x_ref, w_ref, ga_ref, g1_ref, lg_ref, lb_ref, o_ref, *, alpha):
    an = _rms(att_ref[0], ga_ref[...]).astype(BF16)
    aw = att_ref.shape[-1]
    mix = jnp.dot(an, w_ref[0:aw, :], preferred_element_type=F32)
    mix = mix + jnp.dot(rec_ref[0], w_ref[aw:, :], preferred_element_type=F32)
    y = alpha * x_ref[0] + g1_ref[0] * mix
    o_ref[0] = _ln_stats(y) * lg_ref[...] + lb_ref[...]


def _out_projection(att, rec_n, x, w_out, gain_attn, g1, ln_g, ln_b, alpha):
    b, n, d = x.shape
    tm = min(n, 512)
    row = lambda bi, i: (bi, i, 0)
    return pl.pallas_call(
        functools.partial(_outproj_kernel, alpha=alpha),
        grid=(b, n // tm),
        in_specs=[
            pl.BlockSpec((1, tm, att.shape[-1]), row),
            pl.BlockSpec((1, tm, rec_n.shape[-1]), row),
            pl.BlockSpec((1, tm, d), row),
            _const_spec(w_out.shape),
            _const_spec((1, att.shape[-1])),
            pl.BlockSpec((1, 1, d), lambda bi, i: (bi, 0, 0)),
            _const_spec((1, d)),
            _const_spec((1, d)),
        ],
        out_specs=pl.BlockSpec((1, tm, d), row),
        out_shape=jax.ShapeDtypeStruct((b, n, d), F32),
        compiler_params=_params(("arbitrary", "arbitrary")),
        name="out_projection",
    )(att, rec_n, x, w_out, gain_attn.reshape(1, -1), g1, ln_g.reshape(1, -1), ln_b.reshape(1, -1))


def _ffn_kernel(x_ref, sh_ref, sc_ref, g2_ref, wg_ref, wu_ref, wd_ref, lg_ref, lb_ref, o_ref,
                h_ref, *, alpha):
    f = pl.program_id(2)

    @pl.when(f == 0)
    def _():
        h = _ln_stats(x_ref[0]) * (1.0 + sc_ref[0]) + sh_ref[0]
        h_ref[...] = h.astype(BF16)
        o_ref[0] = jnp.zeros(o_ref.shape[1:], F32)

    hb = h_ref[...]
    g = jnp.dot(hb, wg_ref[...], preferred_element_type=F32)
    u = jnp.dot(hb, wu_ref[...], preferred_element_type=F32)
    act = (g * _sigmoid(g) * u).astype(BF16)
    o_ref[0] += jnp.dot(act, wd_ref[...], preferred_element_type=F32)

    @pl.when(f == pl.num_programs(2) - 1)
    def _():
        y = alpha * x_ref[0] + g2_ref[0] * o_ref[0]
        o_ref[0] = _ln_stats(y) * lg_ref[...] + lb_ref[...]


def _dense_ffn(x, shift, scale, g2, wg, wu, wd, ln_g, ln_b, alpha):
    b, n, d = x.shape
    ff = wg.shape[1]
    tm = min(n, 512)
    tf = 512
    row = lambda bi, i, f: (bi, i, 0)
    vec = pl.BlockSpec((1, 1, d), lambda bi, i, f: (bi, 0, 0))
    return pl.pallas_call(
        functools.partial(_ffn_kernel, alpha=alpha),
        grid=(b, n // tm, ff // tf),
        in_specs=[
            pl.BlockSpec((1, tm, d), row), vec, vec, vec,
            pl.BlockSpec((d, tf), lambda bi, i, f: (0, f)),
            pl.BlockSpec((d, tf), lambda bi, i, f: (0, f)),
            pl.BlockSpec((tf, d), lambda bi, i, f: (f, 0)),
            _const_spec((1, d)),
            _const_spec((1, d)),
        ],
        out_specs=pl.BlockSpec((1, tm, d), row),
        out_shape=jax.ShapeDtypeStruct((b, n, d), F32),
        scratch_shapes=[pltpu.VMEM((tm, d), BF16)],
        compiler_params=_params(("arbitrary", "arbitrary", "arbitrary")),
        name="dense_ffn",
    )(x, shift, scale, g2, wg, wu, wd, ln_g.reshape(1, -1), ln_b.reshape(1, -1))


def _router_kernel(x_ref, sh_ref, sc_ref, whi_ref, wlo_ref, h_ref, idx_ref, gate_ref):
    h = _ln_stats(x_ref[0]) * (1.0 + sc_ref[0]) + sh_ref[0]
    h_ref[0] = h
    hi = h.astype(BF16)
    lo = (h - hi.astype(F32)).astype(BF16)
    nt = (((1,), (1,)), ((), ()))
    logits = (lax.dot_general(whi_ref[...], hi, nt, preferred_element_type=F32)
              + lax.dot_general(whi_ref[...], lo, nt, preferred_element_type=F32)
              + lax.dot_general(wlo_ref[...], hi, nt, preferred_element_type=F32))
    e_iota = lax.broadcasted_iota(jnp.int32, logits.shape, 0)
    m1 = jnp.max(logits, axis=0, keepdims=True)
    i1 = jnp.min(jnp.where(logits == m1, e_iota, N_EXPERTS), axis=0, keepdims=True)
    rest = jnp.where(e_iota == i1, -jnp.inf, logits)
    m2 = jnp.max(rest, axis=0, keepdims=True)
    i2 = jnp.min(jnp.where(rest == m2, e_iota, N_EXPERTS), axis=0, keepdims=True)
    e2 = jnp.exp(m2 - m1)
    den = 1.0 + e2
    idx_ref[...] = jnp.where(e_iota == 0, i1, jnp.where(e_iota == 1, i2, 0))
    gate_ref[...] = jnp.where(e_iota == 0, 1.0 / den, jnp.where(e_iota == 1, e2 / den, 0.0))


def _router(x, shift, scale, w_router):
    b, n, d = x.shape
    tm = min(n, 512)
    nb = n // tm
    wt = w_router.T
    w_hi = wt.astype(BF16)
    w_lo = (wt - w_hi.astype(F32)).astype(BF16)
    vec = pl.BlockSpec((1, 1, d), lambda bi, i: (bi, 0, 0))
    col = pl.BlockSpec((N_EXPERTS, tm), lambda bi, i: (0, bi * nb + i))
    return pl.pallas_call(
        _router_kernel,
        grid=(b, nb),
        in_specs=[pl.BlockSpec((1, tm, d), lambda bi, i: (bi, i, 0)), vec, vec,
                  _const_spec(w_hi.shape), _const_spec(w_lo.shape)],
        out_specs=[pl.BlockSpec((1, tm, d), lambda bi, i: (bi, i, 0)), col, col],
        out_shape=[
            jax.ShapeDtypeStruct((b, n, d), F32),
            jax.ShapeDtypeStruct((N_EXPERTS, b * n), jnp.int32),
            jax.ShapeDtypeStruct((N_EXPERTS, b * n), F32),
        ],
        compiler_params=_params(("arbitrary", "arbitrary")),
        name="router",
    )(x, shift, scale, w_hi, w_lo)


def _gather_kernel(src_ref, used_ref, tok_ref, o_ref, sem):
    blk = pl.program_id(0)
    rows = o_ref.shape[0]

    def row_copy(r):
        t = src_ref[blk * rows + r]
        return pltpu.make_async_copy(tok_ref.at[pl.ds(t, 1), :], o_ref.at[pl.ds(r, 1), :], sem)

    @pl.when(blk < used_ref[0])
    def _():
        def start(r, c):
            row_copy(r).start()
            return c

        def wait(r, c):
            row_copy(r).wait()
            return c

        lax.fori_loop(0, rows, start, 0, unroll=8)
        lax.fori_loop(0, rows, wait, 0, unroll=8)

    @pl.when(blk >= used_ref[0])
    def _():
        o_ref[...] = jnp.zeros(o_ref.shape, o_ref.dtype)


def _dispatch_gather(tok, src, n_used, n_blocks):
    n, d = tok.shape
    rows = MOE_ROWS
    return pl.pallas_call(
        _gather_kernel,
        grid_spec=pltpu.PrefetchScalarGridSpec(
            num_scalar_prefetch=2,
            grid=(n_blocks,),
            in_specs=[pl.BlockSpec(memory_space=pl.ANY)],
            out_specs=pl.BlockSpec((rows, d), lambda blk, src, used: (blk, 0)),
            scratch_shapes=[pltpu.SemaphoreType.DMA(())],
        ),
        out_shape=jax.ShapeDtypeStruct((n_blocks * rows, d), tok.dtype),
        compiler_params=_params(("arbitrary",)),
        name="dispatch_gather",
    )(src, n_used, tok)


def _expert_kernel(be_ref, used_ref, x_ref, wg_ref, wu_ref, wd_ref, o_ref, h_ref):
    blk, f = pl.program_id(0), pl.program_id(1)

    @pl.when(blk < used_ref[0])
    def _():
        @pl.when(f == 0)
        def _():
            h_ref[...] = x_ref[...].astype(BF16)
            o_ref[...] = jnp.zeros(o_ref.shape, F32)

        hb = h_ref[...]
        g = jnp.dot(hb, wg_ref[0], preferred_element_type=F32)
        u = jnp.dot(hb, wu_ref[0], preferred_element_type=F32)
        act = (g * _sigmoid(g) * u).astype(BF16)
        o_ref[...] += jnp.dot(act, wd_ref[0], preferred_element_type=F32)

    @pl.when(jnp.logical_and(blk >= used_ref[0], f == 0))
    def _():
        o_ref[...] = jnp.zeros(o_ref.shape, F32)


def _expert_ffn(xs, block_expert, n_used, wg, wu, wd):
    n_slots, d = xs.shape
    rows = MOE_ROWS
    n_blocks = n_slots // rows
    ff = wg.shape[-1]
    tf = 512
    nf = ff // tf

    def blk_of(blk, used):
        return jnp.minimum(blk, used[0] - 1)

    def f_of(blk, f, used):
        return jnp.where(blk < used[0], f, nf - 1)

    return pl.pallas_call(
        _expert_kernel,
        grid_spec=pltpu.PrefetchScalarGridSpec(
            num_scalar_prefetch=2,
            grid=(n_blocks, nf),
            in_specs=[
                pl.BlockSpec((rows, d), lambda blk, f, be, used: (blk_of(blk, used), 0)),
                pl.BlockSpec((1, d, tf), lambda blk, f, be, used: (be[blk_of(blk, used)], 0, f_of(blk, f, used))),
                pl.BlockSpec((1, d, tf), lambda blk, f, be, used: (be[blk_of(blk, used)], 0, f_of(blk, f, used))),
                pl.BlockSpec((1, tf, d), lambda blk, f, be, used: (be[blk_of(blk, used)], f_of(blk, f, used), 0)),
            ],
            out_specs=pl.BlockSpec((rows, d), lambda blk, f, be, used: (blk, 0)),
            scratch_shapes=[pltpu.VMEM((rows, d), BF16)],
        ),
        out_shape=jax.ShapeDtypeStruct((n_slots, d), F32),
        compiler_params=_params(("arbitrary", "arbitrary")),
        name="expert_ffn",
    )(block_expert, n_used, xs, wg, wu, wd)


def _combine_kernel(dest_ref, eo_ref, gate_ref, x_ref, g2_ref, lg_ref, lb_ref, o_ref, buf_ref, sem,
                    *, alpha, blocks_per_batch):
    tm = x_ref.shape[1]
    base = (pl.program_id(0) * blocks_per_batch + pl.program_id(1)) * tm

    def row_copy(r, k):
        slot = dest_ref[(base + r) * TOP_K + k]
        return pltpu.make_async_copy(eo_ref.at[pl.ds(slot, 1), :], buf_ref.at[k, pl.ds(r, 1), :], sem)

    def start(r, c):
        for k in range(TOP_K):
            row_copy(r, k).start()
        return c

    def wait(r, c):
        for k in range(TOP_K):
            row_copy(r, k).wait()
        return c

    lax.fori_loop(0, tm, start, 0, unroll=8)
    lax.fori_loop(0, tm, wait, 0, unroll=8)
    gates = gate_ref[...]
    y = gates[:, 0:1] * buf_ref[0]
    for k in range(1, TOP_K):
        y = y + gates[:, k:k + 1] * buf_ref[k]
    z = alpha * x_ref[0] + g2_ref[0] * y
    o_ref[0] = _ln_stats(z) * lg_ref[...] + lb_ref[...]


def _expert_combine(eo, dest, gates, x, g2, ln_g, ln_b, alpha):
    b, n, d = x.shape
    tm = min(n, 256)
    nb = n // tm
    return pl.pallas_call(
        functools.partial(_combine_kernel, alpha=alpha, blocks_per_batch=nb),
        grid_spec=pltpu.PrefetchScalarGridSpec(
            num_scalar_prefetch=1,
            grid=(b, nb),
            in_specs=[
                pl.BlockSpec(memory_space=pl.ANY),
                pl.BlockSpec((tm, TOP_K), lambda bi, i, dest: (bi * nb + i, 0)),
                pl.BlockSpec((1, tm, d), lambda bi, i, dest: (bi, i, 0)),
                pl.BlockSpec((1, 1, d), lambda bi, i, dest: (bi, 0, 0)),
                pl.BlockSpec((1, d), lambda bi, i, dest: (0, 0)),
                pl.BlockSpec((1, d), lambda bi, i, dest: (0, 0)),
            ],
            out_specs=pl.BlockSpec((1, tm, d), lambda bi, i, dest: (bi, i, 0)),
            scratch_shapes=[pltpu.VMEM((TOP_K, tm, d), F32), pltpu.SemaphoreType.DMA(())],
        ),
        out_shape=jax.ShapeDtypeStruct((b, n, d), F32),
        compiler_params=_params(("arbitrary", "arbitrary")),
        name="expert_combine",
    )(dest, eo, gates, x, g2, ln_g.reshape(1, -1), ln_b.reshape(1, -1))


def _moe(x, shift, scale, g2, w_router, wg, wu, wd, ln_g, ln_b, alpha):
    b, n, d = x.shape
    n_tok = b * n
    h, ridx, rgate = _router(x, shift, scale, w_router)
    exp_a = ridx[:TOP_K].T.reshape(-1)
    gates = rgate[:TOP_K].T
    onehot = (exp_a[:, None] == jnp.arange(N_EXPERTS, dtype=jnp.int32)).astype(jnp.int32)
    csum = jnp.cumsum(onehot, axis=0)
    rank = jnp.take_along_axis(csum, exp_a[:, None], axis=1)[:, 0] - 1
    counts = csum[-1]
    padded = (counts + MOE_ROWS - 1) // MOE_ROWS * MOE_ROWS
    pad_ends = jnp.cumsum(padded)
    pad_starts = pad_ends - padded
    dest = (pad_starts[exp_a] + rank).astype(jnp.int32)
    n_blocks = (n_tok * TOP_K) // MOE_ROWS + N_EXPERTS
    tok_a = jnp.repeat(jnp.arange(n_tok, dtype=jnp.int32), TOP_K)
    src = jnp.zeros((n_blocks * MOE_ROWS,), jnp.int32).at[dest].set(tok_a)
    block_start = jnp.arange(n_blocks, dtype=jnp.int32) * MOE_ROWS
    block_expert = jnp.minimum(jnp.sum(pad_ends[None, :] <= block_start[:, None], axis=1),
                               N_EXPERTS - 1).astype(jnp.int32)
    n_used = (pad_ends[-1:] // MOE_ROWS).astype(jnp.int32)

    xs = _dispatch_gather(h.reshape(n_tok, d), src, n_used, n_blocks)
    eo = _expert_ffn(xs, block_expert, n_used, wg, wu, wd)
    return _expert_combine(eo, dest, gates, x, g2, ln_g, ln_b, alpha)


def _rope_tables(n_tokens):
    half = ROPE_AXIS_DIM // 2
    t = jnp.arange(n_tokens, dtype=jnp.int32)
    pos = jnp.stack([(t // GRID_W).astype(F32), (t % GRID_W).astype(F32)], axis=1)
    inv_freq = ROPE_THETA ** (-jnp.arange(half, dtype=F32) / half)
    ang = pos[:, :, None] * inv_freq
    cos = jnp.cos(ang)
    sin = jnp.sin(ang)
    zero = jnp.zeros_like(sin)
    cos_t = jnp.stack([cos, cos], axis=2).reshape(n_tokens, HEAD_DIM)
    sa = jnp.stack([zero, sin], axis=2).reshape(n_tokens, HEAD_DIM)
    sb = jnp.stack([-sin, zero], axis=2).reshape(n_tokens, HEAD_DIM)
    return cos_t, sa, sb


def _identity_rope(n_tokens):
    one = jnp.ones((n_tokens, HEAD_DIM), F32)
    zero = jnp.zeros((n_tokens, HEAD_DIM), F32)
    return one, zero, zero


def kernel(x, c, ctx, c_ctx, w_mod, b_mod, w_in, q_gain, k_gain, conv_w, conv_b, lru_wa, lru_ba, lru_wx, lru_bx, lru_lambda, out_gain_attn, out_gain_lru, w_out, ln_gain, ln_bias, ffn_w_gate, ffn_w_up, ffn_w_down, router_w, exp_w_gate, exp_w_up, exp_w_down):
    b, s, d = x.shape
    n_ctx = ctx.shape[1]
    depth = w_mod.shape[0]
    lru_w = lru_ba.shape[-1]
    alpha = (2 * depth) ** 0.25

    rope_lat = _rope_tables(s)
    rope_ctx = _identity_rope(n_ctx)

    cc = jnp.concatenate([c, c_ctx[None, :], jnp.zeros((SUBLANES - b - 1, d), F32)], axis=0)
    mod = _modulation(cc, w_mod, b_mod)

    w_in_b = w_in.astype(BF16)
    w_out_b = w_out.astype(BF16)
    w_gates = jnp.concatenate([lru_wa, lru_wx], axis=-1).astype(BF16)
    zero_state = jnp.zeros((b, 1, lru_w), F32)

    for i in range(depth):
        last = i == depth - 1
        lat = [m[:, None, :] for m in jnp.split(mod[i, :b], 6, axis=-1)]
        cmod = [jnp.broadcast_to(m[None, None, :], (b, 1, d)) for m in jnp.split(mod[i, b], 6)]
        sh1, sc1, g1, sh2, sc2, g2 = lat
        csh1, csc1, cg1, csh2, csc2, cg2 = cmod

        q_l, k_l, v_l, u_l, gb_l = _in_projection(x, sh1, sc1, w_in_b[i], q_gain[i], k_gain[i], rope_lat)
        q_c, k_c, v_c, u_c, gb_c = _in_projection(ctx, csh1, csc1, w_in_b[i], q_gain[i], k_gain[i], rope_ctx)

        att_l = _attention(q_l, [(k_c, v_c), (k_l, v_l)])

        lru = lambda u, dr, h0, rev, fused=None: _rglru(
            u, conv_w[i], conv_b[i], w_gates[i, dr], lru_ba[i, dr], lru_bx[i, dr], lru_lambda[i, dr],
            h0, rev, fused)
        hc_f = lru(u_c, 0, zero_state, False)
        hl_f = lru(u_l, 0, hc_f[:, -1:, :], False)
        hc_b = lru(u_c, 1, zero_state, True)
        rec_l = lru(u_l, 1, hc_b[:, 0:1, :], True, (hl_f, gb_l, out_gain_lru[i]))
        if not last:
            rec_c = lru(u_c, 1, zero_state, True, (hc_f, gb_c, out_gain_lru[i]))

        x_mid = _out_projection(att_l, rec_l, x, w_out_b[i], out_gain_attn[i], g1,
                                ln_gain[i, 0], ln_bias[i, 0], alpha)
        if not last:
            att_c = _attention(q_c, [(k_c, v_c)])
            ctx_mid = _out_projection(att_c, rec_c, ctx, w_out_b[i], out_gain_attn[i], cg1,
                                      ln_gain[i, 0], ln_bias[i, 0], alpha)

        j = i // 2
        if i % 2 == 0:
            wg, wu, wd = ffn_w_gate[j].astype(BF16), ffn_w_up[j].astype(BF16), ffn_w_down[j].astype(BF16)
            x = _dense_ffn(x_mid, sh2, sc2, g2, wg, wu, wd, ln_gain[i, 1], ln_bias[i, 1], alpha)
            if not last:
                ctx = _dense_ffn(ctx_mid, csh2, csc2, cg2, wg, wu, wd, ln_gain[i, 1], ln_bias[i, 1], alpha)
        else:
            wg, wu, wd = exp_w_gate[j].astype(BF16), exp_w_up[j].astype(BF16), exp_w_down[j].astype(BF16)
            x = _moe(x_mid, sh2, sc2, g2, router_w[j], wg, wu, wd, ln_gain[i, 1], ln_bias[i, 1], alpha)
            if not last:
                ctx = _moe(ctx_mid, csh2, csc2, cg2, router_w[j], wg, wu, wd,
                           ln_gain[i, 1], ln_bias[i, 1], alpha)
    return x
```

```python
import functools

import jax
import jax.numpy as jnp
from jax import lax
from jax.experimental import pallas as pl
from jax.experimental.pallas import tpu as pltpu

GRID_W = 64
HEAD_DIM = 128
N_KV_HEADS = 2
GQA_GROUP = 4
N_HEADS = N_KV_HEADS * GQA_GROUP
ATTN_W = N_HEADS * HEAD_DIM
KV_W = N_KV_HEADS * HEAD_DIM
LRU_BLOCKS = 8
CONV_W = 4
RG_C = 8.0
ROPE_THETA = 10000.0
ROPE_AXIS_DIM = HEAD_DIM // 2
N_EXPERTS = 8
TOP_K = 2
LN_EPS = 1e-6
LOG2_E = 1.4426950408889634

LANES = 128
SUBLANES = 8
VMEM_LIMIT = 56 * 1024 * 1024
MOE_ROWS = 512
ATTN_TQ = 256
ATTN_TK = 512
F32 = jnp.float32
BF16 = jnp.bfloat16


def _params(sem, vmem=VMEM_LIMIT):
    return pltpu.CompilerParams(dimension_semantics=sem, vmem_limit_bytes=vmem)


def _const_spec(shape):
    nd = len(shape)
    return pl.BlockSpec(shape, lambda *_: (0,) * nd, pipeline_mode=pl.Buffered(1))


def _ln_stats(x):
    mu = jnp.mean(x, axis=-1, keepdims=True)
    xc = x - mu
    var = jnp.mean(xc * xc, axis=-1, keepdims=True)
    return xc * lax.rsqrt(var + LN_EPS)


def _rms(x, gain):
    return x * lax.rsqrt(jnp.mean(x * x, axis=-1, keepdims=True) + LN_EPS) * gain


def _sigmoid(x):
    return 0.5 * jnp.tanh(0.5 * x) + 0.5


def _mod_kernel(c_ref, w_ref, b_ref, o_ref):
    c = c_ref[...]
    s = (c * _sigmoid(c)).astype(BF16)
    o_ref[0] = jnp.dot(s, w_ref[0].astype(BF16), preferred_element_type=F32) + b_ref[0]


def _modulation(cc, w_mod, b_mod):
    depth, d, n = w_mod.shape
    tn = 1024
    rows = cc.shape[0]
    return pl.pallas_call(
        _mod_kernel,
        grid=(depth, n // tn),
        in_specs=[
            pl.BlockSpec((rows, d), lambda l, j: (0, 0)),
            pl.BlockSpec((1, d, tn), lambda l, j: (l, 0, j)),
            pl.BlockSpec((1, 1, tn), lambda l, j: (l, 0, j)),
        ],
        out_specs=pl.BlockSpec((1, rows, tn), lambda l, j: (l, 0, j)),
        out_shape=jax.ShapeDtypeStruct((depth, rows, n), F32),
        compiler_params=_params(("arbitrary", "arbitrary")),
        name="modulation",
    )(cc, w_mod, b_mod.reshape(depth, 1, n))


def _inproj_kernel(x_ref, sh_ref, sc_ref, w_ref, qg_ref, kg_ref, cos_ref, sa_ref, sb_ref,
                   q_ref, k_ref, v_ref, u_ref, gb_ref):
    h = _ln_stats(x_ref[0]) * (1.0 + sc_ref[0]) + sh_ref[0]
    hb = h.astype(BF16)
    cos, sa, sb = cos_ref[...], sa_ref[...], sb_ref[...]

    def norm_rope(xh, gain):
        xn = _rms(xh, gain)
        half = ROPE_AXIS_DIM // 2
        return (xn * cos + pltpu.roll(xn, half, axis=1) * sa
                + pltpu.roll(xn, HEAD_DIM - half, axis=1) * sb)

    q = jnp.dot(hb, w_ref[:, 0:ATTN_W], preferred_element_type=F32)
    q_scale = HEAD_DIM ** -0.5 * LOG2_E
    for hd in range(N_HEADS):
        sl = slice(hd * HEAD_DIM, (hd + 1) * HEAD_DIM)
        q_ref[0, :, sl] = (norm_rope(q[:, sl], qg_ref[...]) * q_scale).astype(q_ref.dtype)
    k = jnp.dot(hb, w_ref[:, ATTN_W:ATTN_W + KV_W], preferred_element_type=F32)
    for hd in range(N_KV_HEADS):
        sl = slice(hd * HEAD_DIM, (hd + 1) * HEAD_DIM)
        k_ref[0, :, sl] = norm_rope(k[:, sl], kg_ref[...]).astype(k_ref.dtype)
    c0 = ATTN_W + KV_W
    v = jnp.dot(hb, w_ref[:, c0:c0 + KV_W], preferred_element_type=F32)
    for hd in range(N_KV_HEADS):
        v_ref[0, :, 2 * hd * HEAD_DIM:(2 * hd + 1) * HEAD_DIM] = (
            v[:, hd * HEAD_DIM:(hd + 1) * HEAD_DIM].astype(v_ref.dtype))
        v_ref[0, :, (2 * hd + 1) * HEAD_DIM:(2 * hd + 2) * HEAD_DIM] = jnp.ones(
            (v.shape[0], HEAD_DIM), v_ref.dtype)
    c0 += KV_W
    lru_w = u_ref.shape[-1]
    u_ref[0] = jnp.dot(hb, w_ref[:, c0:c0 + lru_w], preferred_element_type=F32)
    c0 += lru_w
    gb_ref[0] = jnp.dot(hb, w_ref[:, c0:c0 + lru_w], preferred_element_type=F32)


def _in_projection(x, shift, scale, w_in, q_gain, k_gain, rope):
    b, n, d = x.shape
    lru_w = (w_in.shape[1] - ATTN_W - 2 * KV_W) // 2
    tm = min(n, 512)
    row = lambda bi, i: (bi, i, 0)
    vec = lambda bi, i: (bi, 0, 0)
    tab = pl.BlockSpec((tm, HEAD_DIM), lambda bi, i: (i, 0))
    return pl.pallas_call(
        _inproj_kernel,
        grid=(b, n // tm),
        in_specs=[
            pl.BlockSpec((1, tm, d), row),
            pl.BlockSpec((1, 1, d), vec),
            pl.BlockSpec((1, 1, d), vec),
            _const_spec(w_in.shape),
            _const_spec((1, HEAD_DIM)),
            _const_spec((1, HEAD_DIM)),
            tab, tab, tab,
        ],
        out_specs=[
            pl.BlockSpec((1, tm, ATTN_W), row),
            pl.BlockSpec((1, tm, KV_W), row),
            pl.BlockSpec((1, tm, 2 * KV_W), row),
            pl.BlockSpec((1, tm, lru_w), row),
            pl.BlockSpec((1, tm, lru_w), row),
        ],
        out_shape=[
            jax.ShapeDtypeStruct((b, n, ATTN_W), BF16),
            jax.ShapeDtypeStruct((b, n, KV_W), BF16),
            jax.ShapeDtypeStruct((b, n, 2 * KV_W), BF16),
            jax.ShapeDtypeStruct((b, n, lru_w), F32),
            jax.ShapeDtypeStruct((b, n, lru_w), F32),
        ],
        compiler_params=_params(("arbitrary", "arbitrary")),
        name="in_projection",
    )(x, shift, scale, w_in, q_gain.reshape(1, -1), k_gain.reshape(1, -1), *rope)


def _attn_kernel(*refs, n_src, tk):
    q_ref = refs[0]
    kv_refs = refs[1:1 + 2 * n_src]
    o_ref = refs[1 + 2 * n_src]
    m_ref, acc_ref = refs[2 + 2 * n_src:]
    tq = q_ref.shape[1]
    q = jnp.concatenate(
        [q_ref[0, :, g * HEAD_DIM:(g + 1) * HEAD_DIM] for g in range(GQA_GROUP)], axis=0)
    m_ref[...] = jnp.full(m_ref.shape, -jnp.inf, F32)
    acc_ref[...] = jnp.zeros(acc_ref.shape, F32)

    def update(k, v):
        s = lax.dot_general(q, k, (((1,), (1,)), ((), ())), preferred_element_type=F32)
        m_old = m_ref[...]
        m_new = jnp.maximum(m_old, jnp.max(s, axis=-1, keepdims=True))
        alpha = jnp.exp2(m_old - m_new)
        p = jnp.exp2(s - jnp.concatenate([m_new] * (s.shape[1] // LANES), axis=1))
        pv = jnp.dot(p.astype(BF16), v, preferred_element_type=F32)
        acc_ref[...] = jnp.concatenate([alpha, alpha], axis=1) * acc_ref[...] + pv
        m_ref[...] = m_new

    for si in range(n_src):
        k_ref, v_ref = kv_refs[2 * si], kv_refs[2 * si + 1]
        n_keys = k_ref.shape[1]
        chunk = min(tk, n_keys)

        def body(j, carry, k_ref=k_ref, v_ref=v_ref, chunk=chunk):
            off = pl.multiple_of(j * chunk, chunk)
            update(k_ref[0, pl.ds(off, chunk), :], v_ref[0, pl.ds(off, chunk), :])
            return carry

        n_chunks = n_keys // chunk
        lax.fori_loop(0, n_chunks, body, 0, unroll=8 if n_chunks % 8 == 0 else 1)

    out = acc_ref[:, 0:HEAD_DIM] / acc_ref[:, HEAD_DIM:2 * HEAD_DIM]
    for g in range(GQA_GROUP):
        o_ref[0, :, g * HEAD_DIM:(g + 1) * HEAD_DIM] = out[g * tq:(g + 1) * tq].astype(o_ref.dtype)


def _attention(q, kv_sources):
    b, sq, _ = q.shape
    tq = min(sq, ATTN_TQ)
    gw = GQA_GROUP * HEAD_DIM
    in_specs = [pl.BlockSpec((1, tq, gw), lambda bi, g, i: (bi, i, g))]
    args = [q]
    for k, v in kv_sources:
        in_specs += [pl.BlockSpec((1, k.shape[1], HEAD_DIM), lambda bi, g, i: (bi, 0, g)),
                     pl.BlockSpec((1, k.shape[1], 2 * HEAD_DIM), lambda bi, g, i: (bi, 0, g))]
        args += [k, v]
    rows = GQA_GROUP * tq
    return pl.pallas_call(
        functools.partial(_attn_kernel, n_src=len(kv_sources), tk=ATTN_TK),
        grid=(b, N_KV_HEADS, sq // tq),
        in_specs=in_specs,
        out_specs=pl.BlockSpec((1, tq, gw), lambda bi, g, i: (bi, i, g)),
        out_shape=jax.ShapeDtypeStruct((b, sq, ATTN_W), F32),
        scratch_shapes=[
            pltpu.VMEM((rows, LANES), F32),
            pltpu.VMEM((rows, 2 * HEAD_DIM), F32),
        ],
        compiler_params=_params(("arbitrary", "arbitrary", "arbitrary")),
        name="attention",
    )(*args)


def _lru_kernel(*refs, reverse, fuse, n_chunks):
    (u_ref, up_ref, un_ref, cw_ref, cb_ref, wg_ref, ba_ref, bx_ref, lam_ref, h0_ref) = refs[:10]
    if fuse:
        hf_ref, gb_ref, gain_ref, o_ref, ext_ref, a_ref, b_ref, carry_ref = refs[10:]
    else:
        o_ref, ext_ref, a_ref, b_ref, carry_ref = refs[10:]
    tc, width = u_ref.shape[1], u_ref.shape[2]
    bw = width // LRU_BLOCKS
    j = pl.program_id(1)
    cj = (n_chunks - 1 - j) if reverse else j

    ext_ref[0:SUBLANES, :] = jnp.where(cj == 0, 0.0, up_ref[0])
    ext_ref[SUBLANES:SUBLANES + tc, :] = u_ref[0]
    ext_ref[SUBLANES + tc:, :] = jnp.where(cj == n_chunks - 1, 0.0, un_ref[0])
    pad_l = (CONV_W - 1) // 2
    uc = cb_ref[...]
    for tap in range(CONV_W):
        uc = uc + ext_ref[pl.ds(SUBLANES - pad_l + tap, tc), :] * cw_ref[tap:tap + 1, :]

    for n in range(LRU_BLOCKS):
        sl = slice(n * bw, (n + 1) * bw)
        ub = uc[:, sl]
        g = jnp.dot(ub.astype(BF16), wg_ref[n], preferred_element_type=F32)
        r = _sigmoid(g[:, :bw] + ba_ref[:, sl])
        gi = _sigmoid(g[:, bw:] + bx_ref[:, sl])
        log_a = -RG_C * r * jax.nn.softplus(-lam_ref[:, sl])
        a = jnp.exp(log_a)
        a_ref[:, sl] = a
        x2 = 1.0 - a * a
        b_ref[:, sl] = x2 * lax.rsqrt(jnp.maximum(x2, 1e-30)) * gi * ub

    row = lax.broadcasted_iota(jnp.int32, (SUBLANES, width), 0)
    n_groups = tc // SUBLANES
    last = 0 if reverse else SUBLANES - 1

    def group(gidx, carry):
        gi_ = (n_groups - 1 - gidx) if reverse else gidx
        off = pl.multiple_of(gi_ * SUBLANES, SUBLANES)
        a = a_ref[pl.ds(off, SUBLANES), :]
        bb = b_ref[pl.ds(off, SUBLANES), :]
        for s in (1, 2, 4):
            if reverse:
                keep = row < SUBLANES - s
                shift = SUBLANES - s
            else:
                keep = row >= s
                shift = s
            a_sh = jnp.where(keep, pltpu.roll(a, shift, axis=0), 1.0)
            b_sh = jnp.where(keep, pltpu.roll(bb, shift, axis=0), 0.0)
            bb = a * b_sh + bb
            a = a * a_sh
        h = a * carry + bb
        b_ref[pl.ds(off, SUBLANES), :] = h
        return jnp.broadcast_to(h[last:last + 1, :], (SUBLANES, width))

    @pl.when(j == 0)
    def _():
        carry_ref[...] = jnp.broadcast_to(h0_ref[0], (SUBLANES, width))

    carry_ref[...] = lax.fori_loop(0, n_groups, group, carry_ref[...])

    if fuse:
        rec = (hf_ref[0] + b_ref[...]) * jax.nn.gelu(gb_ref[0])
        o_ref[0] = _rms(rec, gain_ref[...]).astype(o_ref.dtype)
    else:
        o_ref[0] = b_ref[...]


def _rglru(u, conv_w, conv_b, w_gate, ba, bx, lam, h0, reverse, fused=None):
    b, n, width = u.shape
    tc = min(n, 512)
    n_chunks = n // tc
    hb = tc // SUBLANES
    n_rows8 = n // SUBLANES

    def cidx(j):
        return (n_chunks - 1 - j) if reverse else j

    cur = pl.BlockSpec((1, tc, width), lambda bi, j: (bi, cidx(j), 0))
    prev = pl.BlockSpec((1, SUBLANES, width),
                        lambda bi, j: (bi, jnp.maximum(cidx(j) * hb - 1, 0), 0))
    nxt = pl.BlockSpec((1, SUBLANES, width),
                       lambda bi, j: (bi, jnp.minimum((cidx(j) + 1) * hb, n_rows8 - 1), 0))
    vec = _const_spec((1, width))
    in_specs = [cur, prev, nxt, _const_spec((CONV_W, width)), vec, _const_spec(w_gate.shape),
                vec, vec, vec, pl.BlockSpec((1, 1, width), lambda bi, j: (bi, 0, 0))]
    args = [u, u, u, conv_w, conv_b.reshape(1, -1), w_gate, ba.reshape(1, -1), bx.reshape(1, -1),
            lam.reshape(1, -1), h0]
    if fused is not None:
        in_specs += [cur, cur, vec]
        args += [fused[0], fused[1], fused[2].reshape(1, -1)]
    out_dtype = BF16 if fused is not None else F32
    return pl.pallas_call(
        functools.partial(_lru_kernel, reverse=reverse, fuse=fused is not None, n_chunks=n_chunks),
        grid=(b, n_chunks),
        in_specs=in_specs,
        out_specs=cur,
        out_shape=jax.ShapeDtypeStruct((b, n, width), out_dtype),
        scratch_shapes=[
            pltpu.VMEM((tc + 2 * SUBLANES, width), F32),
            pltpu.VMEM((tc, width), F32),
            pltpu.VMEM((tc, width), F32),
            pltpu.VMEM((SUBLANES, width), F32),
        ],
        compiler_params=_params(("arbitrary", "arbitrary")),
        name="rglru_bwd" if reverse else "rglru_fwd",
    )(*args)


def _outproj_kernel(att_ref, rec_ref, x_ref, w_ref, ga_ref, g1_ref, lg_ref, lb_ref, o_ref, *, alpha):
    an = _rms(att_ref[0], ga_ref[...]).astype(BF16)
    aw = att_ref.shape[-1]
    mix = jnp.dot(an, w_ref[0:aw, :], preferred_element_type=F32)
    mix = mix + jnp.dot(rec_ref[0], w_ref[aw:, :], preferred_element_type=F32)
    y = alpha * x_ref[0] + g1_ref[0] * mix
    o_ref[0] = _ln_stats(y) * lg_ref[...] + lb_ref[...]


def _out_projection(att, rec_n, x, w_out, gain_attn, g1, ln_g, ln_b, alpha):
    b, n, d = x.shape
    tm = min(n, 512)
    row = lambda bi, i: (bi, i, 0)
    return pl.pallas_call(
        functools.partial(_outproj_kernel, alpha=alpha),
        grid=(b, n // tm),
        in_specs=[
            pl.BlockSpec((1, tm, att.shape[-1]), row),
            pl.BlockSpec((1, tm, rec_n.shape[-1]), row),
            pl.BlockSpec((1, tm, d), row),
            _const_spec(w_out.shape),
            _const_spec((1, att.shape[-1])),
            pl.BlockSpec((1, 1, d), lambda bi, i: (bi, 0, 0)),
            _const_spec((1, d)),
            _const_spec((1, d)),
        ],
        out_specs=pl.BlockSpec((1, tm, d), row),
        out_shape=jax.ShapeDtypeStruct((b, n, d), F32),
        compiler_params=_params(("arbitrary", "arbitrary")),
        name="out_projection",
    )(att, rec_n, x, w_out, gain_attn.reshape(1, -1), g1, ln_g.reshape(1, -1), ln_b.reshape(1, -1))


def _ffn_kernel(x_ref, sh_ref, sc_ref, g2_ref, wg_ref, wu_ref, wd_ref, lg_ref, lb_ref, o_ref,
                h_ref, *, alpha):
    f = pl.program_id(2)

    @pl.when(f == 0)
    def _():
        h = _ln_stats(x_ref[0]) * (1.0 + sc_ref[0]) + sh_ref[0]
        h_ref[...] = h.astype(BF16)
        o_ref[0] = jnp.zeros(o_ref.shape[1:], F32)

    hb = h_ref[...]
    g = jnp.dot(hb, wg_ref[...], preferred_element_type=F32)
    u = jnp.dot(hb, wu_ref[...], preferred_element_type=F32)
    act = (g * _sigmoid(g) * u).astype(BF16)
    o_ref[0] += jnp.dot(act, wd_ref[...], preferred_element_type=F32)

    @pl.when(f == pl.num_programs(2) - 1)
    def _():
        y = alpha * x_ref[0] + g2_ref[0] * o_ref[0]
        o_ref[0] = _ln_stats(y) * lg_ref[...] + lb_ref[...]


def _dense_ffn(x, shift, scale, g2, wg, wu, wd, ln_g, ln_b, alpha):
    b, n, d = x.shape
    ff = wg.shape[1]
    tm = min(n, 512)
    tf = 512
    row = lambda bi, i, f: (bi, i, 0)
    vec = pl.BlockSpec((1, 1, d), lambda bi, i, f: (bi, 0, 0))
    return pl.pallas_call(
        functools.partial(_ffn_kernel, alpha=alpha),
        grid=(b, n // tm, ff // tf),
        in_specs=[
            pl.BlockSpec((1, tm, d), row), vec, vec, vec,
            pl.BlockSpec((d, tf), lambda bi, i, f: (0, f)),
            pl.BlockSpec((d, tf), lambda bi, i, f: (0, f)),
            pl.BlockSpec((tf, d), lambda bi, i, f: (f, 0)),
            _const_spec((1, d)),
            _const_spec((1, d)),
        ],
        out_specs=pl.BlockSpec((1, tm, d), row),
        out_shape=jax.ShapeDtypeStruct((b, n, d), F32),
        scratch_shapes=[pltpu.VMEM((tm, d), BF16)],
        compiler_params=_params(("arbitrary", "arbitrary", "arbitrary")),
        name="dense_ffn",
    )(x, shift, scale, g2, wg, wu, wd, ln_g.reshape(1, -1), ln_b.reshape(1, -1))


def _router_kernel(x_ref, sh_ref, sc_ref, whi_ref, wlo_ref, h_ref, idx_ref, gate_ref):
    h = _ln_stats(x_ref[0]) * (1.0 + sc_ref[0]) + sh_ref[0]
    h_ref[0] = h
    hi = h.astype(BF16)
    lo = (h - hi.astype(F32)).astype(BF16)
    nt = (((1,), (1,)), ((), ()))
    logits = (lax.dot_general(whi_ref[...], hi, nt, preferred_element_type=F32)
              + lax.dot_general(whi_ref[...], lo, nt, preferred_element_type=F32)
              + lax.dot_general(wlo_ref[...], hi, nt, preferred_element_type=F32))
    e_iota = lax.broadcasted_iota(jnp.int32, logits.shape, 0)
    m1 = jnp.max(logits, axis=0, keepdims=True)
    i1 = jnp.min(jnp.where(logits == m1, e_iota, N_EXPERTS), axis=0, keepdims=True)
    rest = jnp.where(e_iota == i1, -jnp.inf, logits)
    m2 = jnp.max(rest, axis=0, keepdims=True)
    i2 = jnp.min(jnp.where(rest == m2, e_iota, N_EXPERTS), axis=0, keepdims=True)
    e2 = jnp.exp(m2 - m1)
    den = 1.0 + e2
    idx_ref[...] = jnp.where(e_iota == 0, i1, jnp.where(e_iota == 1, i2, 0))
    gate_ref[...] = jnp.where(e_iota == 0, 1.0 / den, jnp.where(e_iota == 1, e2 / den, 0.0))


def _router(x, shift, scale, w_router):
    b, n, d = x.shape
    tm = min(n, 512)
    nb = n // tm
    wt = w_router.T
    w_hi = wt.astype(BF16)
    w_lo = (wt - w_hi.astype(F32)).astype(BF16)
    vec = pl.BlockSpec((1, 1, d), lambda bi, i: (bi, 0, 0))
    col = pl.BlockSpec((N_EXPERTS, tm), lambda bi, i: (0, bi * nb + i))
    return pl.pallas_call(
        _router_kernel,
        grid=(b, nb),
        in_specs=[pl.BlockSpec((1, tm, d), lambda bi, i: (bi, i, 0)), vec, vec,
                  _const_spec(w_hi.shape), _const_spec(w_lo.shape)],
        out_specs=[pl.BlockSpec((1, tm, d), lambda bi, i: (bi, i, 0)), col, col],
        out_shape=[
            jax.ShapeDtypeStruct((b, n, d), F32),
            jax.ShapeDtypeStruct((N_EXPERTS, b * n), jnp.int32),
            jax.ShapeDtypeStruct((N_EXPERTS, b * n), F32),
        ],
        compiler_params=_params(("arbitrary", "arbitrary")),
        name="router",
    )(x, shift, scale, w_hi, w_lo)


def _gather_kernel(src_ref, used_ref, tok_ref, o_ref, sem):
    blk = pl.program_id(0)
    rows = o_ref.shape[0]

    def row_copy(r):
        t = src_ref[blk * rows + r]
        return pltpu.make_async_copy(tok_ref.at[pl.ds(t, 1), :], o_ref.at[pl.ds(r, 1), :], sem)

    @pl.when(blk < used_ref[0])
    def _():
        def start(r, c):
            row_copy(r).start()
            return c

        def wait(r, c):
            row_copy(r).wait()
            return c

        lax.fori_loop(0, rows, start, 0, unroll=8)
        lax.fori_loop(0, rows, wait, 0, unroll=8)

    @pl.when(blk >= used_ref[0])
    def _():
        o_ref[...] = jnp.zeros(o_ref.shape, o_ref.dtype)


def _dispatch_gather(tok, src, n_used, n_blocks):
    n, d = tok.shape
    rows = MOE_ROWS
    return pl.pallas_call(
        _gather_kernel,
        grid_spec=pltpu.PrefetchScalarGridSpec(
            num_scalar_prefetch=2,
            grid=(n_blocks,),
            in_specs=[pl.BlockSpec(memory_space=pl.ANY)],
            out_specs=pl.BlockSpec((rows, d), lambda blk, src, used: (blk, 0)),
            scratch_shapes=[pltpu.SemaphoreType.DMA(())],
        ),
        out_shape=jax.ShapeDtypeStruct((n_blocks * rows, d), tok.dtype),
        compiler_params=_params(("arbitrary",)),
        name="dispatch_gather",
    )(src, n_used, tok)


def _expert_kernel(be_ref, used_ref, x_ref, wg_ref, wu_ref, wd_ref, o_ref, h_ref):
    blk, f = pl.program_id(0), pl.program_id(1)

    @pl.when(blk < used_ref[0])
    def _():
        @pl.when(f == 0)
        def _():
            h_ref[...] = x_ref[...].astype(BF16)
            o_ref[...] = jnp.zeros(o_ref.shape, F32)

        hb = h_ref[...]
        g = jnp.dot(hb, wg_ref[0], preferred_element_type=F32)
        u = jnp.dot(hb, wu_ref[0], preferred_element_type=F32)
        act = (g * _sigmoid(g) * u).astype(BF16)
        o_ref[...] += jnp.dot(act, wd_ref[0], preferred_element_type=F32)

    @pl.when(jnp.logical_and(blk >= used_ref[0], f == 0))
    def _():
        o_ref[...] = jnp.zeros(o_ref.shape, F32)


def _expert_ffn(xs, block_expert, n_used, wg, wu, wd):
    n_slots, d = xs.shape
    rows = MOE_ROWS
    n_blocks = n_slots // rows
    ff = wg.shape[-1]
    tf = 1024 if ff % 1024 == 0 else 512
    nf = ff // tf

    def blk_of(blk, used):
        return jnp.minimum(blk, used[0] - 1)

    def f_of(blk, f, used):
        return jnp.where(blk < used[0], f, nf - 1)

    return pl.pallas_call(
        _expert_kernel,
        grid_spec=pltpu.PrefetchScalarGridSpec(
            num_scalar_prefetch=2,
            grid=(n_blocks, nf),
            in_specs=[
                pl.BlockSpec((rows, d), lambda blk, f, be, used: (blk_of(blk, used), 0)),
                pl.BlockSpec((1, d, tf), lambda blk, f, be, used: (be[blk_of(blk, used)], 0, f_of(blk, f, used))),
                pl.BlockSpec((1, d, tf), lambda blk, f, be, used: (be[blk_of(blk, used)], 0, f_of(blk, f, used))),
                pl.BlockSpec((1, tf, d), lambda blk, f, be, used: (be[blk_of(blk, used)], f_of(blk, f, used), 0)),
            ],
            out_specs=pl.BlockSpec((rows, d), lambda blk, f, be, used: (blk, 0)),
            scratch_shapes=[pltpu.VMEM((rows, d), BF16)],
        ),
        out_shape=jax.ShapeDtypeStruct((n_slots, d), F32),
        compiler_params=_params(("arbitrary", "arbitrary")),
        name="expert_ffn",
    )(block_expert, n_used, xs, wg, wu, wd)


def _combine_kernel(dest_ref, eo_ref, gate_ref, x_ref, g2_ref, lg_ref, lb_ref, o_ref, buf_ref, sem,
                    *, alpha, blocks_per_batch):
    tm = x_ref.shape[1]
    base = (pl.program_id(0) * blocks_per_batch + pl.program_id(1)) * tm

    def row_copy(r, k):
        slot = dest_ref[(base + r) * TOP_K + k]
        return pltpu.make_async_copy(eo_ref.at[pl.ds(slot, 1), :], buf_ref.at[k, pl.ds(r, 1), :], sem)

    def start(r, c):
        for k in range(TOP_K):
            row_copy(r, k).start()
        return c

    def wait(r, c):
        for k in range(TOP_K):
            row_copy(r, k).wait()
        return c

    lax.fori_loop(0, tm, start, 0, unroll=8)
    lax.fori_loop(0, tm, wait, 0, unroll=8)
    gates = gate_ref[...]
    y = gates[:, 0:1] * buf_ref[0]
    for k in range(1, TOP_K):
        y = y + gates[:, k:k + 1] * buf_ref[k]
    z = alpha * x_ref[0] + g2_ref[0] * y
    o_ref[0] = _ln_stats(z) * lg_ref[...] + lb_ref[...]


def _expert_combine(eo, dest, gates, x, g2, ln_g, ln_b, alpha):
    b, n, d = x.shape
    tm = min(n, 256)
    nb = n // tm
    return pl.pallas_call(
        functools.partial(_combine_kernel, alpha=alpha, blocks_per_batch=nb),
        grid_spec=pltpu.PrefetchScalarGridSpec(
            num_scalar_prefetch=1,
            grid=(b, nb),
            in_specs=[
                pl.BlockSpec(memory_space=pl.ANY),
                pl.BlockSpec((tm, TOP_K), lambda bi, i, dest: (bi * nb + i, 0)),
                pl.BlockSpec((1, tm, d), lambda bi, i, dest: (bi, i, 0)),
                pl.BlockSpec((1, 1, d), lambda bi, i, dest: (bi, 0, 0)),
                pl.BlockSpec((1, d), lambda bi, i, dest: (0, 0)),
                pl.BlockSpec((1, d), lambda bi, i, dest: (0, 0)),
            ],
            out_specs=pl.BlockSpec((1, tm, d), lambda bi, i, dest: (bi, i, 0)),
            scratch_shapes=[pltpu.VMEM((TOP_K, tm, d), F32), pltpu.SemaphoreType.DMA(())],
        ),
        out_shape=jax.ShapeDtypeStruct((b, n, d), F32),
        compiler_params=_params(("arbitrary", "arbitrary")),
        name="expert_combine",
    )(dest, eo, gates, x, g2, ln_g.reshape(1, -1), ln_b.reshape(1, -1))


def _moe(x, shift, scale, g2, w_router, wg, wu, wd, ln_g, ln_b, alpha):
    b, n, d = x.shape
    n_tok = b * n
    h, ridx, rgate = _router(x, shift, scale, w_router)
    exp_a = ridx[:TOP_K].T.reshape(-1)
    gates = rgate[:TOP_K].T
    onehot = (exp_a[:, None] == jnp.arange(N_EXPERTS, dtype=jnp.int32)).astype(jnp.int32)
    csum = jnp.cumsum(onehot, axis=0)
    rank = jnp.take_along_axis(csum, exp_a[:, None], axis=1)[:, 0] - 1
    counts = csum[-1]
    padded = (counts + MOE_ROWS - 1) // MOE_ROWS * MOE_ROWS
    pad_ends = jnp.cumsum(padded)
    pad_starts = pad_ends - padded
    dest = (pad_starts[exp_a] + rank).astype(jnp.int32)
    n_blocks = (n_tok * TOP_K) // MOE_ROWS + N_EXPERTS
    tok_a = jnp.repeat(jnp.arange(n_tok, dtype=jnp.int32), TOP_K)
    src = jnp.zeros((n_blocks * MOE_ROWS,), jnp.int32).at[dest].set(tok_a)
    block_start = jnp.arange(n_blocks, dtype=jnp.int32) * MOE_ROWS
    block_expert = jnp.minimum(jnp.sum(pad_ends[None, :] <= block_start[:, None], axis=1),
                               N_EXPERTS - 1).astype(jnp.int32)
    n_used = (pad_ends[-1:] // MOE_ROWS).astype(jnp.int32)

    xs = _dispatch_gather(h.reshape(n_tok, d), src, n_used, n_blocks)
    eo = _expert_ffn(xs, block_expert, n_used, wg, wu, wd)
    return _expert_combine(eo, dest, gates, x, g2, ln_g, ln_b, alpha)


def _rope_tables(n_tokens):
    half = ROPE_AXIS_DIM // 2
    t = jnp.arange(n_tokens, dtype=jnp.int32)
    pos = jnp.stack([(t // GRID_W).astype(F32), (t % GRID_W).astype(F32)], axis=1)
    inv_freq = ROPE_THETA ** (-jnp.arange(half, dtype=F32) / half)
    ang = pos[:, :, None] * inv_freq
    cos = jnp.cos(ang)
    sin = jnp.sin(ang)
    zero = jnp.zeros_like(sin)
    cos_t = jnp.stack([cos, cos], axis=2).reshape(n_tokens, HEAD_DIM)
    sa = jnp.stack([zero, sin], axis=2).reshape(n_tokens, HEAD_DIM)
    sb = jnp.stack([-sin, zero], axis=2).reshape(n_tokens, HEAD_DIM)
    return cos_t, sa, sb


def _identity_rope(n_tokens):
    one = jnp.ones((n_tokens, HEAD_DIM), F32)
    zero = jnp.zeros((n_tokens, HEAD_DIM), F32)
    return one, zero, zero


def kernel(x, c, ctx, c_ctx, w_mod, b_mod, w_in, q_gain, k_gain, conv_w, conv_b, lru_wa, lru_ba, lru_wx, lru_bx, lru_lambda, out_gain_attn, out_gain_lru, w_out, ln_gain, ln_bias, ffn_w_gate, ffn_w_up, ffn_w_down, router_w, exp_w_gate, exp_w_up, exp_w_down):
    b, s, d = x.shape
    n_ctx = ctx.shape[1]
    depth = w_mod.shape[0]
    lru_w = lru_ba.shape[-1]
    alpha = (2 * depth) ** 0.25

    rope_lat = _rope_tables(s)
    rope_ctx = _identity_rope(n_ctx)

    cc = jnp.concatenate([c, c_ctx[None, :], jnp.zeros((SUBLANES - b - 1, d), F32)], axis=0)
    mod = _modulation(cc, w_mod, b_mod)

    w_in_b = w_in.astype(BF16)
    w_out_b = w_out.astype(BF16)
    w_gates = jnp.concatenate([lru_wa, lru_wx], axis=-1).astype(BF16)
    zero_state = jnp.zeros((b, 1, lru_w), F32)

    for i in range(depth):
        last = i == depth - 1
        lat = [m[:, None, :] for m in jnp.split(mod[i, :b], 6, axis=-1)]
        cmod = [jnp.broadcast_to(m[None, None, :], (b, 1, d)) for m in jnp.split(mod[i, b], 6)]
        sh1, sc1, g1, sh2, sc2, g2 = lat
        csh1, csc1, cg1, csh2, csc2, cg2 = cmod

        q_l, k_l, v_l, u_l, gb_l = _in_projection(x, sh1, sc1, w_in_b[i], q_gain[i], k_gain[i], rope_lat)
        q_c, k_c, v_c, u_c, gb_c = _in_projection(ctx, csh1, csc1, w_in_b[i], q_gain[i], k_gain[i], rope_ctx)

        att_l = _attention(q_l, [(k_c, v_c), (k_l, v_l)])

        lru = lambda u, dr, h0, rev, fused=None: _rglru(
            u, conv_w[i], conv_b[i], w_gates[i, dr], lru_ba[i, dr], lru_bx[i, dr], lru_lambda[i, dr],
            h0, rev, fused)
        hc_f = lru(u_c, 0, zero_state, False)
        hl_f = lru(u_l, 0, hc_f[:, -1:, :], False)
        hc_b = lru(u_c, 1, zero_state, True)
        rec_l = lru(u_l, 1, hc_b[:, 0:1, :], True, (hl_f, gb_l, out_gain_lru[i]))
        if not last:
            rec_c = lru(u_c, 1, zero_state, True, (hc_f, gb_c, out_gain_lru[i]))

        x_mid = _out_projection(att_l, rec_l, x, w_out_b[i], out_gain_attn[i], g1,
                                ln_gain[i, 0], ln_bias[i, 0], alpha)
        if not last:
            att_c = _attention(q_c, [(k_c, v_c)])
            ctx_mid = _out_projection(att_c, rec_c, ctx, w_out_b[i], out_gain_attn[i], cg1,
                                      ln_gain[i, 0], ln_bias[i, 0], alpha)

        j = i // 2
        if i % 2 == 0:
            wg, wu, wd = ffn_w_gate[j].astype(BF16), ffn_w_up[j].astype(BF16), ffn_w_down[j].astype(BF16)
            x = _dense_ffn(x_mid, sh2, sc2, g2, wg, wu, wd, ln_gain[i, 1], ln_bias[i, 1], alpha)
            if not last:
                ctx = _dense_ffn(ctx_mid, csh2, csc2, cg2, wg, wu, wd, ln_gain[i, 1], ln_bias[i, 1], alpha)
        else:
            wg, wu, wd = exp_w_gate[j].astype(BF16), exp_w_up[j].astype(BF16), exp_w_down[j].astype(BF16)
            x = _moe(x_mid, sh2, sc2, g2, router_w[j], wg, wu, wd, ln_gain[i, 1], ln_bias[i, 1], alpha)
            if not last:
                ctx = _moe(ctx_mid, csh2, csc2, cg2, router_w[j], wg, wu, wd,
                           ln_gain[i, 1], ln_bias[i, 1], alpha)
    return x
```

```python
import functools

import jax
import jax.numpy as jnp
from jax import lax
from jax.experimental import pallas as pl
from jax.experimental.pallas import tpu as pltpu

GRID_W = 64
HEAD_DIM = 128
N_KV_HEADS = 2
GQA_GROUP = 4
N_HEADS = N_KV_HEADS * GQA_GROUP
ATTN_W = N_HEADS * HEAD_DIM
KV_W = N_KV_HEADS * HEAD_DIM
LRU_BLOCKS = 8
CONV_W = 4
RG_C = 8.0
ROPE_THETA = 10000.0
ROPE_AXIS_DIM = HEAD_DIM // 2
N_EXPERTS = 8
TOP_K = 2
LN_EPS = 1e-6
LOG2_E = 1.4426950408889634

LANES = 128
SUBLANES = 8
VMEM_LIMIT = 56 * 1024 * 1024
MOE_ROWS = 512
ATTN_TQ = 256
ATTN_TK = 512
F32 = jnp.float32
BF16 = jnp.bfloat16


def _params(sem, vmem=VMEM_LIMIT):
    return pltpu.CompilerParams(dimension_semantics=sem, vmem_limit_bytes=vmem)


def _const_spec(shape):
    nd = len(shape)
    return pl.BlockSpec(shape, lambda *_: (0,) * nd, pipeline_mode=pl.Buffered(1))


def _ln_stats(x):
    mu = jnp.mean(x, axis=-1, keepdims=True)
    xc = x - mu
    var = jnp.mean(xc * xc, axis=-1, keepdims=True)
    return xc * lax.rsqrt(var + LN_EPS)


def _rms(x, gain):
    return x * lax.rsqrt(jnp.mean(x * x, axis=-1, keepdims=True) + LN_EPS) * gain


def _sigmoid(x):
    return 0.5 * jnp.tanh(0.5 * x) + 0.5


def _mod_kernel(c_ref, w_ref, b_ref, o_ref):
    c = c_ref[...]
    s = (c * _sigmoid(c)).astype(BF16)
    o_ref[0] = jnp.dot(s, w_ref[0].astype(BF16), preferred_element_type=F32) + b_ref[0]


def _modulation(cc, w_mod, b_mod):
    depth, d, n = w_mod.shape
    tn = 1024
    rows = cc.shape[0]
    return pl.pallas_call(
        _mod_kernel,
        grid=(depth, n // tn),
        in_specs=[
            pl.BlockSpec((rows, d), lambda l, j: (0, 0)),
            pl.BlockSpec((1, d, tn), lambda l, j: (l, 0, j)),
            pl.BlockSpec((1, 1, tn), lambda l, j: (l, 0, j)),
        ],
        out_specs=pl.BlockSpec((1, rows, tn), lambda l, j: (l, 0, j)),
        out_shape=jax.ShapeDtypeStruct((depth, rows, n), F32),
        compiler_params=_params(("arbitrary", "arbitrary")),
        name="modulation",
    )(cc, w_mod, b_mod.reshape(depth, 1, n))


def _inproj_kernel(x_ref, sh_ref, sc_ref, w_ref, qg_ref, kg_ref, cos_ref, sa_ref, sb_ref,
                   q_ref, k_ref, v_ref, u_ref, gb_ref):
    h = _ln_stats(x_ref[0]) * (1.0 + sc_ref[0]) + sh_ref[0]
    hb = h.astype(BF16)
    cos, sa, sb = cos_ref[...], sa_ref[...], sb_ref[...]

    def norm_rope(xh, gain):
        xn = _rms(xh, gain)
        half = ROPE_AXIS_DIM // 2
        return (xn * cos + pltpu.roll(xn, half, axis=1) * sa
                + pltpu.roll(xn, HEAD_DIM - half, axis=1) * sb)

    q = jnp.dot(hb, w_ref[:, 0:ATTN_W], preferred_element_type=F32)
    q_scale = HEAD_DIM ** -0.5 * LOG2_E
    for hd in range(N_HEADS):
        sl = slice(hd * HEAD_DIM, (hd + 1) * HEAD_DIM)
        q_ref[0, :, sl] = (norm_rope(q[:, sl], qg_ref[...]) * q_scale).astype(q_ref.dtype)
    k = jnp.dot(hb, w_ref[:, ATTN_W:ATTN_W + KV_W], preferred_element_type=F32)
    for hd in range(N_KV_HEADS):
        sl = slice(hd * HEAD_DIM, (hd + 1) * HEAD_DIM)
        k_ref[0, :, sl] = norm_rope(k[:, sl], kg_ref[...]).astype(k_ref.dtype)
    c0 = ATTN_W + KV_W
    v = jnp.dot(hb, w_ref[:, c0:c0 + KV_W], preferred_element_type=F32)
    for hd in range(N_KV_HEADS):
        v_ref[0, :, 2 * hd * HEAD_DIM:(2 * hd + 1) * HEAD_DIM] = (
            v[:, hd * HEAD_DIM:(hd + 1) * HEAD_DIM].astype(v_ref.dtype))
        v_ref[0, :, (2 * hd + 1) * HEAD_DIM:(2 * hd + 2) * HEAD_DIM] = jnp.ones(
            (v.shape[0], HEAD_DIM), v_ref.dtype)
    c0 += KV_W
    lru_w = u_ref.shape[-1]
    u_ref[0] = jnp.dot(hb, w_ref[:, c0:c0 + lru_w], preferred_element_type=F32)
    c0 += lru_w
    gb_ref[0] = jnp.dot(hb, w_ref[:, c0:c0 + lru_w], preferred_element_type=F32)


def _in_projection(x, shift, scale, w_in, q_gain, k_gain, rope):
    b, n, d = x.shape
    lru_w = (w_in.shape[1] - ATTN_W - 2 * KV_W) // 2
    tm = min(n, 512)
    row = lambda bi, i: (bi, i, 0)
    vec = lambda bi, i: (bi, 0, 0)
    tab = pl.BlockSpec((tm, HEAD_DIM), lambda bi, i: (i, 0))
    return pl.pallas_call(
        _inproj_kernel,
        grid=(b, n // tm),
        in_specs=[
            pl.BlockSpec((1, tm, d), row),
            pl.BlockSpec((1, 1, d), vec),
            pl.BlockSpec((1, 1, d), vec),
            _const_spec(w_in.shape),
            _const_spec((1, HEAD_DIM)),
            _const_spec((1, HEAD_DIM)),
            tab, tab, tab,
        ],
        out_specs=[
            pl.BlockSpec((1, tm, ATTN_W), row),
            pl.BlockSpec((1, tm, KV_W), row),
            pl.BlockSpec((1, tm, 2 * KV_W), row),
            pl.BlockSpec((1, tm, lru_w), row),
            pl.BlockSpec((1, tm, lru_w), row),
        ],
        out_shape=[
            jax.ShapeDtypeStruct((b, n, ATTN_W), BF16),
            jax.ShapeDtypeStruct((b, n, KV_W), BF16),
            jax.ShapeDtypeStruct((b, n, 2 * KV_W), BF16),
            jax.ShapeDtypeStruct((b, n, lru_w), F32),
            jax.ShapeDtypeStruct((b, n, lru_w), F32),
        ],
        compiler_params=_params(("arbitrary", "arbitrary")),
        name="in_projection",
    )(x, shift, scale, w_in, q_gain.reshape(1, -1), k_gain.reshape(1, -1), *rope)


def _attn_kernel(*refs, n_src, tk):
    q_ref = refs[0]
    kv_refs = refs[1:1 + 2 * n_src]
    o_ref = refs[1 + 2 * n_src]
    m_ref, acc_ref = refs[2 + 2 * n_src:]
    tq = q_ref.shape[1]
    q = jnp.concatenate(
        [q_ref[0, :, g * HEAD_DIM:(g + 1) * HEAD_DIM] for g in range(GQA_GROUP)], axis=0)
    m_ref[...] = jnp.full(m_ref.shape, -jnp.inf, F32)
    acc_ref[...] = jnp.zeros(acc_ref.shape, F32)

    def update(k, v):
        s = lax.dot_general(q, k, (((1,), (1,)), ((), ())), preferred_element_type=F32)
        m_old = m_ref[...]
        m_new = jnp.maximum(m_old, jnp.max(s, axis=-1, keepdims=True))
        alpha = jnp.exp2(m_old - m_new)
        p = jnp.exp2(s - jnp.concatenate([m_new] * (s.shape[1] // LANES), axis=1))
        pv = jnp.dot(p.astype(BF16), v, preferred_element_type=F32)
        acc_ref[...] = jnp.concatenate([alpha, alpha], axis=1) * acc_ref[...] + pv
        m_ref[...] = m_new

    for si in range(n_src):
        k_ref, v_ref = kv_refs[2 * si], kv_refs[2 * si + 1]
        n_keys = k_ref.shape[1]
        chunk = min(tk, n_keys)

        def body(j, carry, k_ref=k_ref, v_ref=v_ref, chunk=chunk):
            off = pl.multiple_of(j * chunk, chunk)
            update(k_ref[0, pl.ds(off, chunk), :], v_ref[0, pl.ds(off, chunk), :])
            return carry

        n_chunks = n_keys // chunk
        lax.fori_loop(0, n_chunks, body, 0, unroll=True)

    out = acc_ref[:, 0:HEAD_DIM] / acc_ref[:, HEAD_DIM:2 * HEAD_DIM]
    for g in range(GQA_GROUP):
        o_ref[0, :, g * HEAD_DIM:(g + 1) * HEAD_DIM] = out[g * tq:(g + 1) * tq].astype(o_ref.dtype)


def _attention(q, kv_sources):
    b, sq, _ = q.shape
    tq = min(sq, ATTN_TQ)
    gw = GQA_GROUP * HEAD_DIM
    in_specs = [pl.BlockSpec((1, tq, gw), lambda bi, g, i: (bi, i, g))]
    args = [q]
    for k, v in kv_sources:
        in_specs += [pl.BlockSpec((1, k.shape[1], HEAD_DIM), lambda bi, g, i: (bi, 0, g)),
                     pl.BlockSpec((1, k.shape[1], 2 * HEAD_DIM), lambda bi, g, i: (bi, 0, g))]
        args += [k, v]
    rows = GQA_GROUP * tq
    return pl.pallas_call(
        functools.partial(_attn_kernel, n_src=len(kv_sources), tk=ATTN_TK),
        grid=(b, N_KV_HEADS, sq // tq),
        in_specs=in_specs,
        out_specs=pl.BlockSpec((1, tq, gw), lambda bi, g, i: (bi, i, g)),
        out_shape=jax.ShapeDtypeStruct((b, sq, ATTN_W), F32),
        scratch_shapes=[
            pltpu.VMEM((rows, LANES), F32),
            pltpu.VMEM((rows, 2 * HEAD_DIM), F32),
        ],
        compiler_params=_params(("arbitrary", "arbitrary", "arbitrary")),
        name="attention",
    )(*args)


def _lru_kernel(*refs, reverse, fuse, n_chunks):
    (u_ref, up_ref, un_ref, cw_ref, cb_ref, wg_ref, ba_ref, bx_ref, lam_ref, h0_ref) = refs[:10]
    if fuse:
        hf_ref, gb_ref, gain_ref, o_ref, ext_ref, a_ref, b_ref, carry_ref = refs[10:]
    else:
        o_ref, ext_ref, a_ref, b_ref, carry_ref = refs[10:]
    tc, width = u_ref.shape[1], u_ref.shape[2]
    bw = width // LRU_BLOCKS
    j = pl.program_id(1)
    cj = (n_chunks - 1 - j) if reverse else j

    ext_ref[0:SUBLANES, :] = jnp.where(cj == 0, 0.0, up_ref[0])
    ext_ref[SUBLANES:SUBLANES + tc, :] = u_ref[0]
    ext_ref[SUBLANES + tc:, :] = jnp.where(cj == n_chunks - 1, 0.0, un_ref[0])
    pad_l = (CONV_W - 1) // 2
    uc = cb_ref[...]
    for tap in range(CONV_W):
        uc = uc + ext_ref[pl.ds(SUBLANES - pad_l + tap, tc), :] * cw_ref[tap:tap + 1, :]

    for n in range(LRU_BLOCKS):
        sl = slice(n * bw, (n + 1) * bw)
        ub = uc[:, sl]
        g = jnp.dot(ub.astype(BF16), wg_ref[n], preferred_element_type=F32)
        r = _sigmoid(g[:, :bw] + ba_ref[:, sl])
        gi = _sigmoid(g[:, bw:] + bx_ref[:, sl])
        log_a = -RG_C * r * jax.nn.softplus(-lam_ref[:, sl])
        a = jnp.exp(log_a)
        a_ref[:, sl] = a
        x2 = 1.0 - a * a
        b_ref[:, sl] = x2 * lax.rsqrt(jnp.maximum(x2, 1e-30)) * gi * ub

    row = lax.broadcasted_iota(jnp.int32, (SUBLANES, width), 0)
    n_groups = tc // SUBLANES
    last = 0 if reverse else SUBLANES - 1

    def group(gidx, carry):
        gi_ = (n_groups - 1 - gidx) if reverse else gidx
        off = pl.multiple_of(gi_ * SUBLANES, SUBLANES)
        a = a_ref[pl.ds(off, SUBLANES), :]
        bb = b_ref[pl.ds(off, SUBLANES), :]
        for s in (1, 2, 4):
            if reverse:
                keep = row < SUBLANES - s
                shift = SUBLANES - s
            else:
                keep = row >= s
                shift = s
            a_sh = jnp.where(keep, pltpu.roll(a, shift, axis=0), 1.0)
            b_sh = jnp.where(keep, pltpu.roll(bb, shift, axis=0), 0.0)
            bb = a * b_sh + bb
            a = a * a_sh
        h = a * carry + bb
        b_ref[pl.ds(off, SUBLANES), :] = h
        return jnp.broadcast_to(h[last:last + 1, :], (SUBLANES, width))

    @pl.when(j == 0)
    def _():
        carry_ref[...] = jnp.broadcast_to(h0_ref[0], (SUBLANES, width))

    carry_ref[...] = lax.fori_loop(0, n_groups, group, carry_ref[...])

    if fuse:
        rec = (hf_ref[0] + b_ref[...]) * jax.nn.gelu(gb_ref[0])
        o_ref[0] = _rms(rec, gain_ref[...]).astype(o_ref.dtype)
    else:
        o_ref[0] = b_ref[...]


def _rglru(u, conv_w, conv_b, w_gate, ba, bx, lam, h0, reverse, fused=None):
    b, n, width = u.shape
    tc = min(n, 512)
    n_chunks = n // tc
    hb = tc // SUBLANES
    n_rows8 = n // SUBLANES

    def cidx(j):
        return (n_chunks - 1 - j) if reverse else j

    cur = pl.BlockSpec((1, tc, width), lambda bi, j: (bi, cidx(j), 0))
    prev = pl.BlockSpec((1, SUBLANES, width),
                        lambda bi, j: (bi, jnp.maximum(cidx(j) * hb - 1, 0), 0))
    nxt = pl.BlockSpec((1, SUBLANES, width),
                       lambda bi, j: (bi, jnp.minimum((cidx(j) + 1) * hb, n_rows8 - 1), 0))
    vec = _const_spec((1, width))
    in_specs = [cur, prev, nxt, _const_spec((CONV_W, width)), vec, _const_spec(w_gate.shape),
                vec, vec, vec, pl.BlockSpec((1, 1, width), lambda bi, j: (bi, 0, 0))]
    args = [u, u, u, conv_w, conv_b.reshape(1, -1), w_gate, ba.reshape(1, -1), bx.reshape(1, -1),
            lam.reshape(1, -1), h0]
    if fused is not None:
        in_specs += [cur, cur, vec]
        args += [fused[0], fused[1], fused[2].reshape(1, -1)]
    out_dtype = BF16 if fused is not None else F32
    return pl.pallas_call(
        functools.partial(_lru_kernel, reverse=reverse, fuse=fused is not None, n_chunks=n_chunks),
        grid=(b, n_chunks),
        in_specs=in_specs,
        out_specs=cur,
        out_shape=jax.ShapeDtypeStruct((b, n, width), out_dtype),
        scratch_shapes=[
            pltpu.VMEM((tc + 2 * SUBLANES, width), F32),
            pltpu.VMEM((tc, width), F32),
            pltpu.VMEM((tc, width), F32),
            pltpu.VMEM((SUBLANES, width), F32),
        ],
        compiler_params=_params(("arbitrary", "arbitrary")),
        name="rglru_bwd" if reverse else "rglru_fwd",
    )(*args)


def _outproj_kernel(att_ref, rec_ref, x_ref, w_ref, ga_ref, g1_ref, lg_ref, lb_ref, o_ref, *, alpha):
    an = _rms(att_ref[0], ga_ref[...]).astype(BF16)
    aw = att_ref.shape[-1]
    mix = jnp.dot(an, w_ref[0:aw, :], preferred_element_type=F32)
    mix = mix + jnp.dot(rec_ref[0], w_ref[aw:, :], preferred_element_type=F32)
    y = alpha * x_ref[0] + g1_ref[0] * mix
    o_ref[0] = _ln_stats(y) * lg_ref[...] + lb_ref[...]


def _out_projection(att, rec_n, x, w_out, gain_attn, g1, ln_g, ln_b, alpha):
    b, n, d = x.shape
    tm = min(n, 512)
    row = lambda bi, i: (bi, i, 0)
    return pl.pallas_call(
        functools.partial(_outproj_kernel, alpha=alpha),
        grid=(b, n // tm),
        in_specs=[
            pl.BlockSpec((1, tm, att.shape[-1]), row),
            pl.BlockSpec((1, tm, rec_n.shape[-1]), row),
            pl.BlockSpec((1, tm, d), row),
            _const_spec(w_out.shape),
            _const_spec((1, att.shape[-1])),
            pl.BlockSpec((1, 1, d), lambda bi, i: (bi, 0, 0)),
            _const_spec((1, d)),
            _const_spec((1, d)),
        ],
        out_specs=pl.BlockSpec((1, tm, d), row),
        out_shape=jax.ShapeDtypeStruct((b, n, d), F32),
        compiler_params=_params(("arbitrary", "arbitrary")),
        name="out_projection",
    )(att, rec_n, x, w_out, gain_attn.reshape(1, -1), g1, ln_g.reshape(1, -1), ln_b.reshape(1, -1))


def _ffn_kernel(x_ref, sh_ref, sc_ref, g2_ref, wg_ref, wu_ref, wd_ref, lg_ref, lb_ref, o_ref,
                h_ref, *, alpha):
    f = pl.program_id(2)

    @pl.when(f == 0)
    def _():
        h = _ln_stats(x_ref[0]) * (1.0 + sc_ref[0]) + sh_ref[0]
        h_ref[...] = h.astype(BF16)
        o_ref[0] = jnp.zeros(o_ref.shape[1:], F32)

    hb = h_ref[...]
    g = jnp.dot(hb, wg_ref[...], preferred_element_type=F32)
    u = jnp.dot(hb, wu_ref[...], preferred_element_type=F32)
    act = (g * _sigmoid(g) * u).astype(BF16)
    o_ref[0] += jnp.dot(act, wd_ref[...], preferred_element_type=F32)

    @pl.when(f == pl.num_programs(2) - 1)
    def _():
        y = alpha * x_ref[0] + g2_ref[0] * o_ref[0]
        o_ref[0] = _ln_stats(y) * lg_ref[...] + lb_ref[...]


def _dense_ffn(x, shift, scale, g2, wg, wu, wd, ln_g, ln_b, alpha):
    b, n, d = x.shape
    ff = wg.shape[1]
    tm = min(n, 512)
    tf = 512
    row = lambda bi, i, f: (bi, i, 0)
    vec = pl.BlockSpec((1, 1, d), lambda bi, i, f: (bi, 0, 0))
    return pl.pallas_call(
        functools.partial(_ffn_kernel, alpha=alpha),
        grid=(b, n // tm, ff // tf),
        in_specs=[
            pl.BlockSpec((1, tm, d), row), vec, vec, vec,
            pl.BlockSpec((d, tf), lambda bi, i, f: (0, f)),
            pl.BlockSpec((d, tf), lambda bi, i, f: (0, f)),
            pl.BlockSpec((tf, d), lambda bi, i, f: (f, 0)),
            _const_spec((1, d)),
            _const_spec((1, d)),
        ],
        out_specs=pl.BlockSpec((1, tm, d), row),
        out_shape=jax.ShapeDtypeStruct((b, n, d), F32),
        scratch_shapes=[pltpu.VMEM((tm, d), BF16)],
        compiler_params=_params(("arbitrary", "arbitrary", "arbitrary")),
        name="dense_ffn",
    )(x, shift, scale, g2, wg, wu, wd, ln_g.reshape(1, -1), ln_b.reshape(1, -1))


def _router_kernel(x_ref, sh_ref, sc_ref, whi_ref, wlo_ref, h_ref, idx_ref, gate_ref):
    h = _ln_stats(x_ref[0]) * (1.0 + sc_ref[0]) + sh_ref[0]
    h_ref[0] = h
    hi = h.astype(BF16)
    lo = (h - hi.astype(F32)).astype(BF16)
    nt = (((1,), (1,)), ((), ()))
    logits = (lax.dot_general(whi_ref[...], hi, nt, preferred_element_type=F32)
              + lax.dot_general(whi_ref[...], lo, nt, preferred_element_type=F32)
              + lax.dot_general(wlo_ref[...], hi, nt, preferred_element_type=F32))
    e_iota = lax.broadcasted_iota(jnp.int32, logits.shape, 0)
    m1 = jnp.max(logits, axis=0, keepdims=True)
    i1 = jnp.min(jnp.where(logits == m1, e_iota, N_EXPERTS), axis=0, keepdims=True)
    rest = jnp.where(e_iota == i1, -jnp.inf, logits)
    m2 = jnp.max(rest, axis=0, keepdims=True)
    i2 = jnp.min(jnp.where(rest == m2, e_iota, N_EXPERTS), axis=0, keepdims=True)
    e2 = jnp.exp(m2 - m1)
    den = 1.0 + e2
    idx_ref[...] = jnp.where(e_iota == 0, i1, jnp.where(e_iota == 1, i2, 0))
    gate_ref[...] = jnp.where(e_iota == 0, 1.0 / den, jnp.where(e_iota == 1, e2 / den, 0.0))


def _router(x, shift, scale, w_router):
    b, n, d = x.shape
    tm = min(n, 512)
    nb = n // tm
    wt = w_router.T
    w_hi = wt.astype(BF16)
    w_lo = (wt - w_hi.astype(F32)).astype(BF16)
    vec = pl.BlockSpec((1, 1, d), lambda bi, i: (bi, 0, 0))
    col = pl.BlockSpec((N_EXPERTS, tm), lambda bi, i: (0, bi * nb + i))
    return pl.pallas_call(
        _router_kernel,
        grid=(b, nb),
        in_specs=[pl.BlockSpec((1, tm, d), lambda bi, i: (bi, i, 0)), vec, vec,
                  _const_spec(w_hi.shape), _const_spec(w_lo.shape)],
        out_specs=[pl.BlockSpec((1, tm, d), lambda bi, i: (bi, i, 0)), col, col],
        out_shape=[
            jax.ShapeDtypeStruct((b, n, d), F32),
            jax.ShapeDtypeStruct((N_EXPERTS, b * n), jnp.int32),
            jax.ShapeDtypeStruct((N_EXPERTS, b * n), F32),
        ],
        compiler_params=_params(("arbitrary", "arbitrary")),
        name="router",
    )(x, shift, scale, w_hi, w_lo)


def _gather_kernel(src_ref, used_ref, tok_ref, o_ref, sem):
    blk = pl.program_id(0)
    rows = o_ref.shape[0]

    def row_copy(r):
        t = src_ref[blk * rows + r]
        return pltpu.make_async_copy(tok_ref.at[pl.ds(t, 1), :], o_ref.at[pl.ds(r, 1), :], sem)

    @pl.when(blk < used_ref[0])
    def _():
        def start(r, c):
            row_copy(r).start()
            return c

        def wait(r, c):
            row_copy(r).wait()
            return c

        lax.fori_loop(0, rows, start, 0, unroll=8)
        lax.fori_loop(0, rows, wait, 0, unroll=8)

    @pl.when(blk >= used_ref[0])
    def _():
        o_ref[...] = jnp.zeros(o_ref.shape, o_ref.dtype)


def _dispatch_gather(tok, src, n_used, n_blocks):
    n, d = tok.shape
    rows = MOE_ROWS
    return pl.pallas_call(
        _gather_kernel,
        grid_spec=pltpu.PrefetchScalarGridSpec(
            num_scalar_prefetch=2,
            grid=(n_blocks,),
            in_specs=[pl.BlockSpec(memory_space=pl.ANY)],
            out_specs=pl.BlockSpec((rows, d), lambda blk, src, used: (blk, 0)),
            scratch_shapes=[pltpu.SemaphoreType.DMA(())],
        ),
        out_shape=jax.ShapeDtypeStruct((n_blocks * rows, d), tok.dtype),
        compiler_params=_params(("arbitrary",)),
        name="dispatch_gather",
    )(src, n_used, tok)


def _expert_kernel(be_ref, used_ref, x_ref, wg_ref, wu_ref, wd_ref, o_ref, h_ref):
    blk, f = pl.program_id(0), pl.program_id(1)

    @pl.when(blk < used_ref[0])
    def _():
        @pl.when(f == 0)
        def _():
            h_ref[...] = x_ref[...].astype(BF16)
            o_ref[...] = jnp.zeros(o_ref.shape, F32)

        hb = h_ref[...]
        g = jnp.dot(hb, wg_ref[0], preferred_element_type=F32)
        u = jnp.dot(hb, wu_ref[0], preferred_element_type=F32)
        act = (g * _sigmoid(g) * u).astype(BF16)
        o_ref[...] += jnp.dot(act, wd_ref[0], preferred_element_type=F32)

    @pl.when(jnp.logical_and(blk >= used_ref[0], f == 0))
    def _():
        o_ref[...] = jnp.zeros(o_ref.shape, F32)


def _expert_ffn(xs, block_expert, n_used, wg, wu, wd):
    n_slots, d = xs.shape
    rows = MOE_ROWS
    n_blocks = n_slots // rows
    ff = wg.shape[-1]
    tf = 1024 if ff % 1024 == 0 else 512
    nf = ff // tf

    def blk_of(blk, used):
        return jnp.minimum(blk, used[0] - 1)

    def f_of(blk, f, used):
        return jnp.where(blk < used[0], f, nf - 1)

    return pl.pallas_call(
        _expert_kernel,
        grid_spec=pltpu.PrefetchScalarGridSpec(
            num_scalar_prefetch=2,
            grid=(n_blocks, nf),
            in_specs=[
                pl.BlockSpec((rows, d), lambda blk, f, be, used: (blk_of(blk, used), 0)),
                pl.BlockSpec((1, d, tf), lambda blk, f, be, used: (be[blk_of(blk, used)], 0, f_of(blk, f, used))),
                pl.BlockSpec((1, d, tf), lambda blk, f, be, used: (be[blk_of(blk, used)], 0, f_of(blk, f, used))),
                pl.BlockSpec((1, tf, d), lambda blk, f, be, used: (be[blk_of(blk, used)], f_of(blk, f, used), 0)),
            ],
            out_specs=pl.BlockSpec((rows, d), lambda blk, f, be, used: (blk, 0)),
            scratch_shapes=[pltpu.VMEM((rows, d), BF16)],
        ),
        out_shape=jax.ShapeDtypeStruct((n_slots, d), F32),
        compiler_params=_params(("arbitrary", "arbitrary")),
        name="expert_ffn",
    )(block_expert, n_used, xs, wg, wu, wd)


def _combine_kernel(dest_ref, eo_ref, gate_ref, x_ref, g2_ref, lg_ref, lb_ref, o_ref, buf_ref, sem,
                    *, alpha, blocks_per_batch):
    tm = x_ref.shape[1]
    base = (pl.program_id(0) * blocks_per_batch + pl.program_id(1)) * tm

    def row_copy(r, k):
        slot = dest_ref[(base + r) * TOP_K + k]
        return pltpu.make_async_copy(eo_ref.at[pl.ds(slot, 1), :], buf_ref.at[k, pl.ds(r, 1), :], sem)

    def start(r, c):
        for k in range(TOP_K):
            row_copy(r, k).start()
        return c

    def wait(r, c):
        for k in range(TOP_K):
            row_copy(r, k).wait()
        return c

    lax.fori_loop(0, tm, start, 0, unroll=8)
    lax.fori_loop(0, tm, wait, 0, unroll=8)
    gates = gate_ref[...]
    y = gates[:, 0:1] * buf_ref[0]
    for k in range(1, TOP_K):
        y = y + gates[:, k:k + 1] * buf_ref[k]
    z = alpha * x_ref[0] + g2_ref[0] * y
    o_ref[0] = _ln_stats(z) * lg_ref[...] + lb_ref[...]


def _expert_combine(eo, dest, gates, x, g2, ln_g, ln_b, alpha):
    b, n, d = x.shape
    tm = min(n, 256)
    nb = n // tm
    return pl.pallas_call(
        functools.partial(_combine_kernel, alpha=alpha, blocks_per_batch=nb),
        grid_spec=pltpu.PrefetchScalarGridSpec(
            num_scalar_prefetch=1,
            grid=(b, nb),
            in_specs=[
                pl.BlockSpec(memory_space=pl.ANY),
                pl.BlockSpec((tm, TOP_K), lambda bi, i, dest: (bi * nb + i, 0)),
                pl.BlockSpec((1, tm, d), lambda bi, i, dest: (bi, i, 0)),
                pl.BlockSpec((1, 1, d), lambda bi, i, dest: (bi, 0, 0)),
                pl.BlockSpec((1, d), lambda bi, i, dest: (0, 0)),
                pl.BlockSpec((1, d), lambda bi, i, dest: (0, 0)),
            ],
            out_specs=pl.BlockSpec((1, tm, d), lambda bi, i, dest: (bi, i, 0)),
            scratch_shapes=[pltpu.VMEM((TOP_K, tm, d), F32), pltpu.SemaphoreType.DMA(())],
        ),
        out_shape=jax.ShapeDtypeStruct((b, n, d), F32),
        compiler_params=_params(("arbitrary", "arbitrary")),
        name="expert_combine",
    )(dest, eo, gates, x, g2, ln_g.reshape(1, -1), ln_b.reshape(1, -1))


def _moe(x, shift, scale, g2, w_router, wg, wu, wd, ln_g, ln_b, alpha):
    b, n, d = x.shape
    n_tok = b * n
    h, ridx, rgate = _router(x, shift, scale, w_router)
    exp_a = ridx[:TOP_K].T.reshape(-1)
    gates = rgate[:TOP_K].T
    onehot = (exp_a[:, None] == jnp.arange(N_EXPERTS, dtype=jnp.int32)).astype(jnp.int32)
    csum = jnp.cumsum(onehot, axis=0)
    rank = jnp.take_along_axis(csum, exp_a[:, None], axis=1)[:, 0] - 1
    counts = csum[-1]
    padded = (counts + MOE_ROWS - 1) // MOE_ROWS * MOE_ROWS
    pad_ends = jnp.cumsum(padded)
    pad_starts = pad_ends - padded
    dest = (pad_starts[exp_a] + rank).astype(jnp.int32)
    n_blocks = (n_tok * TOP_K) // MOE_ROWS + N_EXPERTS
    tok_a = jnp.repeat(jnp.arange(n_tok, dtype=jnp.int32), TOP_K)
    src = jnp.zeros((n_blocks * MOE_ROWS,), jnp.int32).at[dest].set(tok_a)
    block_start = jnp.arange(n_blocks, dtype=jnp.int32) * MOE_ROWS
    block_expert = jnp.minimum(jnp.sum(pad_ends[None, :] <= block_start[:, None], axis=1),
                               N_EXPERTS - 1).astype(jnp.int32)
    n_used = (pad_ends[-1:] // MOE_ROWS).astype(jnp.int32)

    xs = _dispatch_gather(h.reshape(n_tok, d), src, n_used, n_blocks)
    eo = _expert_ffn(xs, block_expert, n_used, wg, wu, wd)
    return _expert_combine(eo, dest, gates, x, g2, ln_g, ln_b, alpha)


def _rope_tables(n_tokens):
    half = ROPE_AXIS_DIM // 2
    t = jnp.arange(n_tokens, dtype=jnp.int32)
    pos = jnp.stack([(t // GRID_W).astype(F32), (t % GRID_W).astype(F32)], axis=1)
    inv_freq = ROPE_THETA ** (-jnp.arange(half, dtype=F32) / half)
    ang = pos[:, :, None] * inv_freq
    cos = jnp.cos(ang)
    sin = jnp.sin(ang)
    zero = jnp.zeros_like(sin)
    cos_t = jnp.stack([cos, cos], axis=2).reshape(n_tokens, HEAD_DIM)
    sa = jnp.stack([zero, sin], axis=2).reshape(n_tokens, HEAD_DIM)
    sb = jnp.stack([-sin, zero], axis=2).reshape(n_tokens, HEAD_DIM)
    return cos_t, sa, sb


def _identity_rope(n_tokens):
    one = jnp.ones((n_tokens, HEAD_DIM), F32)
    zero = jnp.zeros((n_tokens, HEAD_DIM), F32)
    return one, zero, zero


def kernel(x, c, ctx, c_ctx, w_mod, b_mod, w_in, q_gain, k_gain, conv_w, conv_b, lru_wa, lru_ba, lru_wx, lru_bx, lru_lambda, out_gain_attn, out_gain_lru, w_out, ln_gain, ln_bias, ffn_w_gate, ffn_w_up, ffn_w_down, router_w, exp_w_gate, exp_w_up, exp_w_down):
    b, s, d = x.shape
    n_ctx = ctx.shape[1]
    depth = w_mod.shape[0]
    lru_w = lru_ba.shape[-1]
    alpha = (2 * depth) ** 0.25

    rope_lat = _rope_tables(s)
    rope_ctx = _identity_rope(n_ctx)

    cc = jnp.concatenate([c, c_ctx[None, :], jnp.zeros((SUBLANES - b - 1, d), F32)], axis=0)
    mod = _modulation(cc, w_mod, b_mod)

    w_in_b = w_in.astype(BF16)
    w_out_b = w_out.astype(BF16)
    w_gates = jnp.concatenate([lru_wa, lru_wx], axis=-1).astype(BF16)
    zero_state = jnp.zeros((b, 1, lru_w), F32)

    for i in range(depth):
        last = i == depth - 1
        lat = [m[:, None, :] for m in jnp.split(mod[i, :b], 6, axis=-1)]
        cmod = [jnp.broadcast_to(m[None, None, :], (b, 1, d)) for m in jnp.split(mod[i, b], 6)]
        sh1, sc1, g1, sh2, sc2, g2 = lat
        csh1, csc1, cg1, csh2, csc2, cg2 = cmod

        q_l, k_l, v_l, u_l, gb_l = _in_projection(x, sh1, sc1, w_in_b[i], q_gain[i], k_gain[i], rope_lat)
        q_c, k_c, v_c, u_c, gb_c = _in_projection(ctx, csh1, csc1, w_in_b[i], q_gain[i], k_gain[i], rope_ctx)

        att_l = _attention(q_l, [(k_c, v_c), (k_l, v_l)])

        lru = lambda u, dr, h0, rev, fused=None: _rglru(
            u, conv_w[i], conv_b[i], w_gates[i, dr], lru_ba[i, dr], lru_bx[i, dr], lru_lambda[i, dr],
            h0, rev, fused)
        hc_f = lru(u_c, 0, zero_state, False)
        hl_f = lru(u_l, 0, hc_f[:, -1:, :], False)
        hc_b = lru(u_c, 1, zero_state, True)
        rec_l = lru(u_l, 1, hc_b[:, 0:1, :], True, (hl_f, gb_l, out_gain_lru[i]))
        if not last:
            rec_c = lru(u_c, 1, zero_state, True, (hc_f, gb_c, out_gain_lru[i]))

        x_mid = _out_projection(att_l, rec_l, x, w_out_b[i], out_gain_attn[i], g1,
                                ln_gain[i, 0], ln_bias[i, 0], alpha)
        if not last:
            att_c = _attention(q_c, [(k_c, v_c)])
            ctx_mid = _out_projection(att_c, rec_c, ctx, w_out_b[i], out_gain_attn[i], cg1,
                                      ln_gain[i, 0], ln_bias[i, 0], alpha)

        j = i // 2
        if i % 2 == 0:
            wg, wu, wd = ffn_w_gate[j].astype(BF16), ffn_w_up[j].astype(BF16), ffn_w_down[j].astype(BF16)
            x = _dense_ffn(x_mid, sh2, sc2, g2, wg, wu, wd, ln_gain[i, 1], ln_bias[i, 1], alpha)
            if not last:
                ctx = _dense_ffn(ctx_mid, csh2, csc2, cg2, wg, wu, wd, ln_gain[i, 1], ln_bias[i, 1], alpha)
        else:
            wg, wu, wd = exp_w_gate[j].astype(BF16), exp_w_up[j].astype(BF16), exp_w_down[j].astype(BF16)
            x = _moe(x_mid, sh2, sc2, g2, router_w[j], wg, wu, wd, ln_gain[i, 1], ln_bias[i, 1], alpha)
            if not last:
                ctx = _moe(ctx_mid, csh2, csc2, cg2, router_w[j], wg, wu, wd,
                           ln_gain[i, 1], ln_bias[i, 1], alpha)
    return x
```
